```python
import math
import jax, jax.numpy as jnp
from jax import lax
import numpy as np

D_MODEL = 4096
BATCH = 4
SEQ = 2048
DEPTH = 2
DEC_BATCH = 8
DEC_SEQ = 2048
PAST_LEN = 128

GRID_W = 64
HEAD_DIM = 128
NA_HEADS = 12
NA_ROWS = 8
NA_COLS = 16
SW_HEADS = 12
SW_KV_HEADS = 4
SW_WINDOW = 128
SW_BLOCK = 128
T5_BUCKETS = 32
T5_MAX_DIST = 128
MEM_TOKENS = 256
MEM_HEADS = 4
MEM_HEAD_DIM = 256
NA_WIDTH = NA_HEADS * HEAD_DIM
SW_WIDTH = SW_HEADS * HEAD_DIM
SW_KV_WIDTH = SW_KV_HEADS * HEAD_DIM
MEM_WIDTH = MEM_HEADS * MEM_HEAD_DIM
IN_SPLITS = (NA_WIDTH, NA_WIDTH, NA_WIDTH, NA_WIDTH,
             SW_WIDTH, SW_KV_WIDTH, SW_KV_WIDTH, SW_WIDTH,
             MEM_WIDTH, MEM_WIDTH,
             D_MODEL, D_MODEL, D_MODEL)
IN_WIDTH = sum(IN_SPLITS)
RMS_EPS = 1e-6
NEG_INF = -1e30

kernel_name = 'hybrid_natten_swa_memory_encoder'


def rms_norm(x, g):
    xf = x.astype(jnp.float32)
    y = xf * lax.rsqrt(jnp.mean(xf * xf, axis=-1, keepdims=True) + RMS_EPS)
    return (y * g.astype(jnp.float32)).astype(x.dtype)


def t5_buckets(rel):
    nb = T5_BUCKETS // 2
    max_exact = nb // 2
    ret = (rel > 0).astype(np.int32) * nb
    n = np.abs(rel)
    large = max_exact + (np.log(np.maximum(n, 1) / max_exact)
                         / np.log(T5_MAX_DIST / max_exact) * (nb - max_exact)).astype(np.int32)
    large = np.minimum(large, nb - 1)
    return (ret + np.where(n < max_exact, n, large)).astype(np.int32)


def neighborhood_attention(q, k, v, rpb):
    B, L, H, dh = q.shape
    R = L // GRID_W
    kr = min(NA_ROWS, R)
    ncb = GRID_W // NA_COLS
    qc = NA_COLS
    kw = 2 * NA_COLS
    qcol = np.arange(GRID_W).reshape(ncb, qc)
    cstart = np.clip(qcol - NA_COLS // 2, 0, GRID_W - NA_COLS)
    kstart = np.clip(np.arange(ncb) * qc - NA_COLS // 2, 0, GRID_W - kw)
    kcol = kstart[:, None] + np.arange(kw)
    col_mask = (kcol[:, None, :] >= cstart[:, :, None]) & (kcol[:, None, :] < cstart[:, :, None] + NA_COLS)
    dc_idx = np.clip(kcol[:, None, :] - qcol[:, :, None], -(NA_COLS - 1), NA_COLS - 1) + NA_COLS - 1
    scale = HEAD_DIM ** -0.5
    qg = (q * scale).reshape(B, R, ncb, qc, H, dh)
    kg = k.reshape(B, R, GRID_W, H, dh)[:, :, kcol]
    vg = v.reshape(B, R, GRID_W, H, dh)[:, :, kcol]
    mask = jnp.asarray(col_mask)[:, :, None, :]
    dc_j = jnp.asarray(dc_idx)[:, :, None, :]

    def row_step(args):
        q_r, r = args
        rs = jnp.clip(r - kr // 2, 0, R - kr)
        k_r = lax.dynamic_slice_in_dim(kg, rs, kr, axis=1)
        v_r = lax.dynamic_slice_in_dim(vg, rs, kr, axis=1)
        dr_idx = (rs + jnp.arange(kr) - r + NA_ROWS - 1)[None, None, :, None]
        bias = rpb[:, dr_idx, dc_j].astype(jnp.float32)
        s = jnp.einsum('bnqhd,bmnkhd->bhnqmk', q_r, k_r).astype(jnp.float32) + bias
        s = jnp.where(mask, s, NEG_INF)
        p = jax.nn.softmax(s.reshape(B, H, ncb, qc, kr * kw), axis=-1).reshape(s.shape).astype(v.dtype)
        return jnp.einsum('bhnqmk,bmnkhd->bnqhd', p, v_r)

    out = lax.map(row_step, (jnp.moveaxis(qg, 1, 0), jnp.arange(R)))
    return jnp.moveaxis(out, 0, 1).reshape(B, L, H * dh)


def window_attention(q, k, v, sink, t5_table):
    B, L, H, dh = q.shape
    G = H // SW_KV_HEADS
    nblk = L // SW_BLOCK
    qb = (q * HEAD_DIM ** -0.5).reshape(B, nblk, SW_BLOCK, SW_KV_HEADS, G, dh)
    pad = ((0, 0), (SW_BLOCK, SW_BLOCK), (0, 0), (0, 0))
    kp = jnp.pad(k, pad)
    vp = jnp.pad(v, pad)
    rel = np.arange(3 * SW_BLOCK)[None, :] - SW_BLOCK - np.arange(SW_BLOCK)[:, None]
    band = jnp.asarray(np.abs(rel) <= SW_WINDOW)
    bias = jnp.transpose(t5_table[jnp.asarray(t5_buckets(rel))], (2, 0, 1)).astype(jnp.float32)
    bias = bias.reshape(SW_KV_HEADS, G, SW_BLOCK, 3 * SW_BLOCK)
    sk = sink.astype(jnp.float32).reshape(SW_KV_HEADS, G)[None, :, :, None, None]

    def block_step(args):
        q_i, i = args
        k_i = lax.dynamic_slice_in_dim(kp, i * SW_BLOCK, 3 * SW_BLOCK, axis=1)
        v_i = lax.dynamic_slice_in_dim(vp, i * SW_BLOCK, 3 * SW_BLOCK, axis=1)
        kpos = i * SW_BLOCK - SW_BLOCK + jnp.arange(3 * SW_BLOCK)
        valid = band & ((kpos >= 0) & (kpos < L))[None, :]
        s = jnp.einsum('bqkgd,bmkd->bkgqm', q_i, k_i).astype(jnp.float32) + bias
        s = jnp.where(valid, s, NEG_INF)
        mx = jnp.maximum(jnp.max(s, axis=-1, keepdims=True), sk)
        e = jnp.exp(s - mx)
        p = (e / (jnp.sum(e, axis=-1, keepdims=True) + jnp.exp(sk - mx))).astype(v.dtype)
        return jnp.einsum('bkgqm,bmkd->bqkgd', p, v_i)

    out = lax.map(block_step, (jnp.moveaxis(qb, 1, 0), jnp.arange(nblk)))
    return jnp.moveaxis(out, 0, 1).reshape(B, L, H * dh)


def memory_attention(q, mem_h, w_kv):
    B, L = q.shape[0], q.shape[1]
    M = mem_h.shape[1]
    k, v = jnp.split(mem_h @ w_kv, 2, axis=-1)
    k = k.reshape(B, M, MEM_HEADS, MEM_HEAD_DIM)
    v = v.reshape(B, M, MEM_HEADS, MEM_HEAD_DIM)
    s = jnp.einsum('blhd,bmhd->bhlm', q * MEM_HEAD_DIM ** -0.5, k).astype(jnp.float32)
    p = jax.nn.softmax(s, axis=-1).astype(v.dtype)
    return jnp.einsum('bhlm,bmhd->blhd', p, v).reshape(B, L, MEM_WIDTH)


def trunk(x, mem, pre_norm, post_norm, mem_norm, w_in, w_mem_kv, w_branch_a, w_branch_b,
          w_branch_m, w_out, na_rpb, attn_sink, t5_bias):
    B, L, _ = x.shape
    split_points = [int(c) for c in np.cumsum(IN_SPLITS)[:-1]]
    for l in range(DEPTH):
        h = rms_norm(x, pre_norm[l])
        (qa, ka, va, za, qb, kb, vb, zb, qm, zm, ga, gb, gm) = jnp.split(h @ w_in[l], split_points, axis=-1)
        a = neighborhood_attention(qa.reshape(B, L, NA_HEADS, HEAD_DIM), ka.reshape(B, L, NA_HEADS, HEAD_DIM),
                                   va.reshape(B, L, NA_HEADS, HEAD_DIM), na_rpb[l]) * jax.nn.silu(za)
        b = window_attention(qb.reshape(B, L, SW_HEADS, HEAD_DIM), kb.reshape(B, L, SW_KV_HEADS, HEAD_DIM),
                             vb.reshape(B, L, SW_KV_HEADS, HEAD_DIM), attn_sink[l], t5_bias) * jax.nn.silu(zb)
        m = memory_attention(qm.reshape(B, L, MEM_HEADS, MEM_HEAD_DIM), rms_norm(mem, mem_norm[l]),
                             w_mem_kv[l]) * jax.nn.silu(zm)
        merged = (jax.nn.sigmoid(ga) * (a @ w_branch_a[l])
                  + jax.nn.sigmoid(gb) * (b @ w_branch_b[l])
                  + jax.nn.sigmoid(gm) * (m @ w_branch_m[l]))
        x = x + rms_norm(merged @ w_out[l], post_norm[l])
    return x


def setup_inputs(seed: int = 0) -> dict:
    key = jax.random.key(seed)
    ks = jax.random.split(key, 16)

    def nrm(k, shape, s):
        return jax.random.normal(k, shape, jnp.float32) * s

    return {
        'x_prompt': nrm(ks[0], (BATCH, SEQ, D_MODEL), 1.0),
        'x_sample': nrm(ks[1], (DEC_BATCH, DEC_SEQ, D_MODEL), 1.0),
        'mem_prompt': nrm(ks[2], (BATCH, MEM_TOKENS, D_MODEL), 1.0),
        'mem_sample': nrm(ks[3], (DEC_BATCH, MEM_TOKENS, D_MODEL), 1.0),
        'pre_norm': 1.0 + nrm(ks[4], (DEPTH, D_MODEL), 0.05),
        'post_norm': 1.0 + nrm(ks[5], (DEPTH, D_MODEL), 0.05),
        'mem_norm': 1.0 + nrm(ks[6], (DEPTH, D_MODEL), 0.05),
        'w_in': nrm(ks[7], (DEPTH, D_MODEL, IN_WIDTH), D_MODEL ** -0.5),
        'w_mem_kv': nrm(ks[8], (DEPTH, D_MODEL, 2 * MEM_WIDTH), D_MODEL ** -0.5),
        'w_branch_a': nrm(ks[9], (DEPTH, NA_WIDTH, D_MODEL), NA_WIDTH ** -0.5),
        'w_branch_b': nrm(ks[10], (DEPTH, SW_WIDTH, D_MODEL), SW_WIDTH ** -0.5),
        'w_branch_m': nrm(ks[11], (DEPTH, MEM_WIDTH, D_MODEL), MEM_WIDTH ** -0.5),
        'w_out': nrm(ks[12], (DEPTH, D_MODEL, D_MODEL), D_MODEL ** -0.5),
        'na_rpb': nrm(ks[13], (DEPTH, NA_HEADS, 2 * NA_ROWS - 1, 2 * NA_COLS - 1), 0.1),
        'attn_sink': nrm(ks[14], (DEPTH, SW_HEADS), 0.5),
        't5_bias': nrm(ks[15], (T5_BUCKETS, SW_HEADS), 0.1),
    }


def reference(x_prompt, x_sample, mem_prompt, mem_sample, pre_norm, post_norm, mem_norm, w_in, w_mem_kv,
              w_branch_a, w_branch_b, w_branch_m, w_out, na_rpb, attn_sink, t5_bias):
    y_prompt = trunk(x_prompt, mem_prompt, pre_norm, post_norm, mem_norm, w_in, w_mem_kv, w_branch_a,
                     w_branch_b, w_branch_m, w_out, na_rpb, attn_sink, t5_bias)
    y_sample = trunk(x_sample, mem_sample, pre_norm, post_norm, mem_norm, w_in, w_mem_kv, w_branch_a,
                     w_branch_b, w_branch_m, w_out, na_rpb, attn_sink, t5_bias)
    return (y_prompt, y_sample)
```

```python
import functools

import numpy as np
import jax
import jax.numpy as jnp
from jax import lax
from jax.experimental import pallas as pl
from jax.experimental.pallas import tpu as pltpu

D_MODEL = 4096
DEPTH = 2
GRID_W = 64
HEAD_DIM = 128
NA_HEADS = 12
NA_ROWS = 8
NA_COLS = 16
SW_HEADS = 12
SW_KV_HEADS = 4
SW_GROUP = SW_HEADS // SW_KV_HEADS
SW_WINDOW = 128
SW_BLOCK = 128
T5_BUCKETS = 32
T5_MAX_DIST = 128
MEM_HEADS = 4
MEM_HEAD_DIM = 256
NA_WIDTH = NA_HEADS * HEAD_DIM
SW_WIDTH = SW_HEADS * HEAD_DIM
SW_KV_WIDTH = SW_KV_HEADS * HEAD_DIM
MEM_WIDTH = MEM_HEADS * MEM_HEAD_DIM
IN_SPLITS = (NA_WIDTH, NA_WIDTH, NA_WIDTH, NA_WIDTH,
             SW_WIDTH, SW_KV_WIDTH, SW_KV_WIDTH, SW_WIDTH,
             MEM_WIDTH, MEM_WIDTH,
             D_MODEL, D_MODEL, D_MODEL)
IN_WIDTH = sum(IN_SPLITS)
(OFF_QA, OFF_KA, OFF_VA, OFF_ZA, OFF_QB, OFF_KB, OFF_VB, OFF_ZB,
 OFF_QM, OFF_ZM, OFF_GA, OFF_GB, OFF_GM) = [int(c) for c in np.cumsum((0,) + IN_SPLITS[:-1])]
RMS_EPS = 1e-6
NEG_INF = -1e30

LANES = 128
VMEM_LIMIT_BYTES = 56 * 1024 * 1024
NORM_ROWS = 256
MM_BM = 1024
MM_BN = 1024
MERGE_BM = 1024
MERGE_BN = 512
OUT_BM = 256
OUT_BN = 512
NA_QROWS = 4
NA_KROWS = 12
MEM_QBLOCK = 512

_NT_DIMS = (((1,), (1,)), ((), ()))


def _params(*semantics):
    return pltpu.CompilerParams(dimension_semantics=semantics, vmem_limit_bytes=VMEM_LIMIT_BYTES)


def _sigmoid(x):
    return 1.0 / (1.0 + jnp.exp(-x))


def _rmsnorm_kernel(x_ref, g_ref, o_ref):
    x = x_ref[...]
    ms = jnp.mean(x * x, axis=-1, keepdims=True)
    o_ref[...] = ((x * lax.rsqrt(ms + RMS_EPS)) * g_ref[...]).astype(o_ref.dtype)


def _rmsnorm(x2d, gain):
    rows, d = x2d.shape
    return pl.pallas_call(
        _rmsnorm_kernel,
        grid=(rows // NORM_ROWS,),
        in_specs=[pl.BlockSpec((NORM_ROWS, d), lambda i: (i, 0)),
                  pl.BlockSpec((1, d), lambda i: (0, 0))],
        out_specs=pl.BlockSpec((NORM_ROWS, d), lambda i: (i, 0)),
        out_shape=jax.ShapeDtypeStruct((rows, d), jnp.bfloat16),
        compiler_params=_params("parallel"),
        name="rmsnorm",
    )(x2d, gain.reshape(1, d))


def _matmul_kernel(x_ref, w_ref, s_ref, o_ref):
    acc = jnp.dot(x_ref[...], w_ref[...], preferred_element_type=jnp.float32)
    o_ref[...] = (acc * s_ref[...]).astype(o_ref.dtype)


def _matmul(x2d, w, col_scale):
    m, k = x2d.shape
    n = w.shape[1]
    bm = min(MM_BM, m)
    bn = min(MM_BN, n)
    return pl.pallas_call(
        _matmul_kernel,
        grid=(m // bm, n // bn),
        in_specs=[pl.BlockSpec((bm, k), lambda i, j: (i, 0)),
                  pl.BlockSpec((k, bn), lambda i, j: (0, j)),
                  pl.BlockSpec((1, bn), lambda i, j: (0, j))],
        out_specs=pl.BlockSpec((bm, bn), lambda i, j: (i, j)),
        out_shape=jax.ShapeDtypeStruct((m, n), jnp.bfloat16),
        compiler_params=_params("parallel", "arbitrary"),
        name="proj_matmul",
    )(x2d, w, col_scale)


def _na_kernel(q_ref, k_ref, v_ref, z_ref, bias_ref, o_ref):
    seq = q_ref.shape[0]
    qb = NA_QROWS * GRID_W
    kb = NA_KROWS * GRID_W
    nblk = seq // qb
    grid_rows = seq // GRID_W

    def body(i, carry):
        case = jnp.where(i == 0, 0, jnp.where(i == nblk - 1, 2, 1))
        krow0 = jnp.clip(NA_QROWS * i - NA_ROWS // 2, 0, grid_rows - NA_KROWS)
        q0 = pl.multiple_of(i * qb, qb)
        k0 = pl.multiple_of(krow0 * GRID_W, GRID_W)
        q = q_ref[pl.ds(q0, qb), :]
        k = k_ref[pl.ds(k0, kb), :]
        v = v_ref[pl.ds(k0, kb), :]
        s = lax.dot_general(q, k, _NT_DIMS, preferred_element_type=jnp.float32) + bias_ref[case]
        mx = jnp.max(s, axis=-1, keepdims=True)
        e = jnp.exp(s - mx)
        den = jnp.sum(e, axis=-1, keepdims=True)
        o = jnp.dot(e.astype(v.dtype), v, preferred_element_type=jnp.float32) / den
        z = z_ref[pl.ds(q0, qb), :].astype(jnp.float32)
        o_ref[pl.ds(q0, qb), :] = (o * (z * _sigmoid(z))).astype(o_ref.dtype)
        return carry

    lax.fori_loop(0, nblk, body, 0)


def _na_attention(proj, bias):
    bsz, seq, _ = proj.shape
    qb = NA_QROWS * GRID_W
    kb = NA_KROWS * GRID_W

    def col(off):
        return pl.BlockSpec((None, seq, HEAD_DIM), lambda h, b: (b, 0, off // HEAD_DIM + h))

    return pl.pallas_call(
        _na_kernel,
        grid=(NA_HEADS, bsz),
        in_specs=[col(OFF_QA), col(OFF_KA), col(OFF_VA), col(OFF_ZA),
                  pl.BlockSpec((None, 3, qb, kb), lambda h, b: (h, 0, 0, 0))],
        out_specs=pl.BlockSpec((None, seq, HEAD_DIM), lambda h, b: (b, 0, h)),
        out_shape=jax.ShapeDtypeStruct((bsz, seq, NA_WIDTH), jnp.bfloat16),
        compiler_params=_params("parallel", "arbitrary"),
        name="na_attention",
    )(proj, proj, proj, proj, bias)


def _na_bias_tables(rpb, seq):
    grid_rows = seq // GRID_W
    nblk = grid_rows // NA_QROWS
    win_rows = min(NA_ROWS, grid_rows)
    qc = np.arange(GRID_W)
    cstart = np.clip(qc - NA_COLS // 2, 0, GRID_W - NA_COLS)
    kc = np.arange(GRID_W)
    col_ok = (kc[None, :] >= cstart[:, None]) & (kc[None, :] < cstart[:, None] + NA_COLS)
    dc = np.clip(kc[None, :] - qc[:, None], -(NA_COLS - 1), NA_COLS - 1) + NA_COLS - 1
    row_ok, dr = [], []
    for i in (0, 1, nblk - 1):
        qr = NA_QROWS * i + np.arange(NA_QROWS)
        kr = np.clip(NA_QROWS * i - NA_ROWS // 2, 0, grid_rows - NA_KROWS) + np.arange(NA_KROWS)
        rs = np.clip(qr - win_rows // 2, 0, grid_rows - win_rows)
        ok = (kr[None, :] >= rs[:, None]) & (kr[None, :] < rs[:, None] + win_rows)
        row_ok.append(ok)
        dr.append(np.clip(kr[None, :] - qr[:, None] + NA_ROWS - 1, 0, 2 * NA_ROWS - 2))
    row_ok = np.stack(row_ok)
    dr = np.stack(dr)
    row_hot = (dr[..., None] == np.arange(2 * NA_ROWS - 1)).astype(np.float32)
    col_hot = (dc[..., None] == np.arange(2 * NA_COLS - 1)).astype(np.float32)
    mask = row_ok[:, :, None, :, None] & col_ok[None, None, :, None, :]
    t = jnp.einsum("sabd,lhde,xye->lhsaxby", jnp.asarray(row_hot), rpb.astype(jnp.float32),
                   jnp.asarray(col_hot), precision=lax.Precision.HIGHEST)
    t = jnp.where(jnp.asarray(mask)[None, None], t, NEG_INF)
    return t.reshape(rpb.shape[0], rpb.shape[1], 3, NA_QROWS * GRID_W, NA_KROWS * GRID_W)


def _sw_kernel(q_ref, k_ref, v_ref, z0_ref, z1_ref, z2_ref, bias_ref, sink_ref, o_ref, kp_ref, vp_ref):
    z_refs = (z0_ref, z1_ref, z2_ref)
    seq = k_ref.shape[0]
    blk = SW_BLOCK
    nblk = seq // blk
    pad = jnp.zeros((blk, HEAD_DIM), kp_ref.dtype)
    kp_ref[0:blk, :] = pad
    vp_ref[0:blk, :] = pad
    kp_ref[seq + blk:seq + 2 * blk, :] = pad
    vp_ref[seq + blk:seq + 2 * blk, :] = pad
    kp_ref[blk:seq + blk, :] = k_ref[...]
    vp_ref[blk:seq + blk, :] = v_ref[...]
    sk = sink_ref[...]

    def body(i, carry):
        r0 = pl.multiple_of(i * blk, blk)
        q = q_ref[pl.ds(r0, blk), :]
        qs = jnp.concatenate([q[:, g * HEAD_DIM:(g + 1) * HEAD_DIM] for g in range(SW_GROUP)], axis=0)
        kw = kp_ref[pl.ds(r0, 3 * blk), :]
        vw = vp_ref[pl.ds(r0, 3 * blk), :]
        s = lax.dot_general(qs, kw, _NT_DIMS, preferred_element_type=jnp.float32) + bias_ref[...]
        kpos = r0 - blk + lax.broadcasted_iota(jnp.int32, (1, 3 * blk), 1)
        s = jnp.where((kpos >= 0) & (kpos < seq), s, NEG_INF)
        mx = jnp.maximum(jnp.max(s, axis=-1, keepdims=True), sk)
        e = jnp.exp(s - mx)
        den = jnp.sum(e, axis=-1, keepdims=True) + jnp.exp(sk - mx)
        o = jnp.dot(e.astype(vw.dtype), vw, preferred_element_type=jnp.float32) / den
        og = jnp.concatenate([o[g * blk:(g + 1) * blk, :] for g in range(SW_GROUP)], axis=1)
        z = jnp.concatenate([r[pl.ds(r0, blk), :] for r in z_refs], axis=1).astype(jnp.float32)
        o_ref[pl.ds(r0, blk), :] = (og * (z * _sigmoid(z))).astype(o_ref.dtype)
        return carry

    lax.fori_loop(0, nblk, body, 0)


def _sw_attention(proj, bias, sink_col):
    bsz, seq, _ = proj.shape
    gw = SW_GROUP * HEAD_DIM

    def wide(off):
        return pl.BlockSpec((None, seq, gw), lambda k, b: (b, 0, off // gw + k))

    def narrow(off):
        return pl.BlockSpec((None, seq, HEAD_DIM), lambda k, b: (b, 0, off // HEAD_DIM + k))

    def gate(g):
        return pl.BlockSpec((None, seq, HEAD_DIM),
                            lambda k, b: (b, 0, OFF_ZB // HEAD_DIM + k * SW_GROUP + g))

    assert OFF_QB % gw == 0 and SW_GROUP == 3
    return pl.pallas_call(
        _sw_kernel,
        grid=(SW_KV_HEADS, bsz),
        in_specs=[wide(OFF_QB), narrow(OFF_KB), narrow(OFF_VB), gate(0), gate(1), gate(2),
                  pl.BlockSpec((SW_GROUP * SW_BLOCK, 3 * SW_BLOCK), lambda k, b: (k, 0)),
                  pl.BlockSpec((SW_GROUP * SW_BLOCK, 1), lambda k, b: (k, 0))],
        out_specs=pl.BlockSpec((None, seq, gw), lambda k, b: (b, 0, k)),
        out_shape=jax.ShapeDtypeStruct((bsz, seq, SW_WIDTH), jnp.bfloat16),
        scratch_shapes=[pltpu.VMEM((seq + 2 * SW_BLOCK, HEAD_DIM), jnp.bfloat16),
                        pltpu.VMEM((seq + 2 * SW_BLOCK, HEAD_DIM), jnp.bfloat16)],
        compiler_params=_params("parallel", "arbitrary"),
        name="sw_attention",
    )(proj, proj, proj, proj, proj, proj, bias, sink_col)


def _t5_bucket_index(rel):
    nb = T5_BUCKETS // 2
    max_exact = nb // 2
    ret = (rel > 0).astype(np.int32) * nb
    n = np.abs(rel)
    large = max_exact + (np.log(np.maximum(n, 1) / max_exact)
                         / np.log(T5_MAX_DIST / max_exact) * (nb - max_exact)).astype(np.int32)
    large = np.minimum(large, nb - 1)
    return (ret + np.where(n < max_exact, n, large)).astype(np.int32)


def _sw_bias_table(t5_bias):
    rel = np.arange(3 * SW_BLOCK)[None, :] - SW_BLOCK - np.arange(SW_BLOCK)[:, None]
    band = np.abs(rel) <= SW_WINDOW
    hot = (_t5_bucket_index(rel)[..., None] == np.arange(T5_BUCKETS)).astype(np.float32)
    t = jnp.einsum("qkb,bh->hqk", jnp.asarray(hot), t5_bias.astype(jnp.float32),
                   precision=lax.Precision.HIGHEST)
    t = jnp.where(jnp.asarray(band)[None], t, NEG_INF)
    return t.reshape(SW_HEADS * SW_BLOCK, 3 * SW_BLOCK)


def _mem_kernel(q_ref, z_ref, k_ref, v_ref, o_ref):
    seq = q_ref.shape[0]
    k = k_ref[...]
    v = v_ref[...]

    def body(i, carry):
        r0 = pl.multiple_of(i * MEM_QBLOCK, MEM_QBLOCK)
        q = q_ref[pl.ds(r0, MEM_QBLOCK), :]
        s = lax.dot_general(q, k, _NT_DIMS, preferred_element_type=jnp.float32)
        mx = jnp.max(s, axis=-1, keepdims=True)
        e = jnp.exp(s - mx)
        den = jnp.sum(e, axis=-1, keepdims=True)
        o = jnp.dot(e.astype(v.dtype), v, preferred_element_type=jnp.float32) / den
        z = z_ref[pl.ds(r0, MEM_QBLOCK), :].astype(jnp.float32)
        o_ref[pl.ds(r0, MEM_QBLOCK), :] = (o * (z * _sigmoid(z))).astype(o_ref.dtype)
        return carry

    lax.fori_loop(0, seq // MEM_QBLOCK, body, 0)


def _mem_attention(proj, memkv):
    bsz, seq, _ = proj.shape
    mtok = memkv.shape[1]

    def qcol(off):
        return pl.BlockSpec((None, seq, MEM_HEAD_DIM), lambda h, b: (b, 0, off // MEM_HEAD_DIM + h))

    def kvcol(off):
        return pl.BlockSpec((None, mtok, MEM_HEAD_DIM), lambda h, b: (b, 0, off // MEM_HEAD_DIM + h))

    assert OFF_QM % MEM_HEAD_DIM == 0 and OFF_ZM % MEM_HEAD_DIM == 0
    return pl.pallas_call(
        _mem_kernel,
        grid=(MEM_HEADS, bsz),
        in_specs=[qcol(OFF_QM), qcol(OFF_ZM), kvcol(0), kvcol(MEM_WIDTH)],
        out_specs=pl.BlockSpec((None, seq, MEM_HEAD_DIM), lambda h, b: (b, 0, h)),
        out_shape=jax.ShapeDtypeStruct((bsz, seq, MEM_WIDTH), jnp.bfloat16),
        compiler_params=_params("parallel", "arbitrary"),
        name="mem_attention",
    )(proj, proj, memkv, memkv)


def _merge_kernel(a_ref, b_ref, m_ref, wa_ref, wb_ref, wm_ref, ga_ref, gb_ref, gm_ref, o_ref):
    def branch(x_ref, w_ref, g_ref):
        y = jnp.dot(x_ref[...], w_ref[...], preferred_element_type=jnp.float32)
        return _sigmoid(g_ref[...].astype(jnp.float32)) * y

    acc = branch(a_ref, wa_ref, ga_ref)
    acc = acc + branch(b_ref, wb_ref, gb_ref)
    acc = acc + branch(m_ref, wm_ref, gm_ref)
    o_ref[...] = acc.astype(o_ref.dtype)


def _merge(a2d, b2d, m2d, wa, wb, wm, proj2d):
    rows = a2d.shape[0]
    bm = min(MERGE_BM, rows)
    bn = MERGE_BN

    def act(width):
        return pl.BlockSpec((bm, width), lambda i, j: (i, 0))

    def wgt(width):
        return pl.BlockSpec((width, bn), lambda i, j: (0, j))

    def gate(off):
        return pl.BlockSpec((bm, bn), lambda i, j: (i, off // bn + j))

    assert OFF_GA % bn == 0 and OFF_GB % bn == 0 and OFF_GM % bn == 0 and rows % bm == 0
    return pl.pallas_call(
        _merge_kernel,
        grid=(rows // bm, D_MODEL // bn),
        in_specs=[act(NA_WIDTH), act(SW_WIDTH), act(MEM_WIDTH),
                  wgt(NA_WIDTH), wgt(SW_WIDTH), wgt(MEM_WIDTH),
                  gate(OFF_GA), gate(OFF_GB), gate(OFF_GM)],
        out_specs=pl.BlockSpec((bm, bn), lambda i, j: (i, j)),
        out_shape=jax.ShapeDtypeStruct((rows, D_MODEL), jnp.bfloat16),
        compiler_params=_params("parallel", "arbitrary"),
        name="branch_merge",
    )(a2d, b2d, m2d, wa, wb, wm, proj2d, proj2d, proj2d)


def _outproj_kernel(mg_ref, w_ref, x_ref, g_ref, o_ref):
    j = pl.program_id(1)
    bn = w_ref.shape[1]
    col = pl.multiple_of(j * bn, bn)
    o_ref[:, pl.ds(col, bn)] = jnp.dot(mg_ref[...], w_ref[...], preferred_element_type=jnp.float32)

    @pl.when(j == pl.num_programs(1) - 1)
    def _():
        y = o_ref[...]
        ms = jnp.mean(y * y, axis=-1, keepdims=True)
        o_ref[...] = x_ref[...] + (y * lax.rsqrt(ms + RMS_EPS)) * g_ref[...]


def _outproj(merged2d, w, x2d, gain):
    rows, d = x2d.shape
    bm = min(OUT_BM, rows)
    bn = OUT_BN
    return pl.pallas_call(
        _outproj_kernel,
        grid=(rows // bm, d // bn),
        in_specs=[pl.BlockSpec((bm, d), lambda i, j: (i, 0)),
                  pl.BlockSpec((d, bn), lambda i, j: (0, j)),
                  pl.BlockSpec((bm, d), lambda i, j: (i, 0)),
                  pl.BlockSpec((1, d), lambda i, j: (0, 0))],
        out_specs=pl.BlockSpec((bm, d), lambda i, j: (i, 0)),
        out_shape=jax.ShapeDtypeStruct((rows, d), jnp.float32),
        compiler_params=_params("parallel", "arbitrary"),
        name="out_proj_norm",
    )(merged2d, w, x2d, gain.reshape(1, d))


def _in_proj_scale():
    s = np.ones((1, IN_WIDTH), np.float32)
    s[0, OFF_QA:OFF_QA + NA_WIDTH] = HEAD_DIM ** -0.5
    s[0, OFF_QB:OFF_QB + SW_WIDTH] = HEAD_DIM ** -0.5
    s[0, OFF_QM:OFF_QM + MEM_WIDTH] = MEM_HEAD_DIM ** -0.5
    return s


def _trunk(x, mem, p):
    bsz, seq, d = x.shape
    mtok = mem.shape[1]
    x2d = x.reshape(bsz * seq, d)
    mem2d = mem.reshape(bsz * mtok, d)
    for l in range(DEPTH):
        h = _rmsnorm(x2d, p["pre_norm"][l])
        proj2d = _matmul(h, p["w_in"][l], p["in_scale"])
        proj = proj2d.reshape(bsz, seq, IN_WIDTH)
        mem_h = _rmsnorm(mem2d, p["mem_norm"][l])
        memkv = _matmul(mem_h, p["w_mem_kv"][l], p["kv_scale"]).reshape(bsz, mtok, 2 * MEM_WIDTH)
        a = _na_attention(proj, p["na_bias"][l])
        b = _sw_attention(proj, p["sw_bias"], p["sink_col"][l])
        m = _mem_attention(proj, memkv)
        merged = _merge(a.reshape(bsz * seq, NA_WIDTH), b.reshape(bsz * seq, SW_WIDTH),
                        m.reshape(bsz * seq, MEM_WIDTH), p["w_branch_a"][l], p["w_branch_b"][l],
                        p["w_branch_m"][l], proj2d)
        x2d = _outproj(merged, p["w_out"][l], x2d, p["post_norm"][l])
    return x2d.reshape(bsz, seq, d)


def kernel(x_prompt, x_sample, mem_prompt, mem_sample, pre_norm, post_norm, mem_norm, w_in, w_mem_kv,
           w_branch_a, w_branch_b, w_branch_m, w_out, na_rpb, attn_sink, t5_bias):
    bf16 = jnp.bfloat16
    seq = x_prompt.shape[1]
    assert x_sample.shape[1] == seq and seq % (NA_QROWS * GRID_W) == 0 and seq % SW_BLOCK == 0
    assert seq // GRID_W >= NA_KROWS and seq % MEM_QBLOCK == 0
    sink_col = jnp.repeat(attn_sink.astype(jnp.float32), SW_BLOCK, axis=1)[..., None]
    p = {
        "pre_norm": pre_norm, "post_norm": post_norm, "mem_norm": mem_norm,
        "w_in": w_in.astype(bf16), "w_mem_kv": w_mem_kv.astype(bf16),
        "w_branch_a": w_branch_a.astype(bf16), "w_branch_b": w_branch_b.astype(bf16),
        "w_branch_m": w_branch_m.astype(bf16), "w_out": w_out.astype(bf16),
        "in_scale": jnp.asarray(_in_proj_scale()),
        "kv_scale": jnp.ones((1, 2 * MEM_WIDTH), jnp.float32),
        "na_bias": _na_bias_tables(na_rpb, seq),
        "sw_bias": _sw_bias_table(t5_bias),
        "sink_col": sink_col,
    }
    y_prompt = _trunk(x_prompt, mem_prompt, p)
    y_sample = _trunk(x_sample, mem_sample, p)
    return (y_prompt, y_sample)
```

```python
import functools

import numpy as np
import jax
import jax.numpy as jnp
from jax import lax
from jax.experimental import pallas as pl
from jax.experimental.pallas import tpu as pltpu

D_MODEL = 4096
DEPTH = 2
GRID_W = 64
HEAD_DIM = 128
NA_HEADS = 12
NA_ROWS = 8
NA_COLS = 16
SW_HEADS = 12
SW_KV_HEADS = 4
SW_GROUP = SW_HEADS // SW_KV_HEADS
SW_WINDOW = 128
SW_BLOCK = 128
T5_BUCKETS = 32
T5_MAX_DIST = 128
MEM_HEADS = 4
MEM_HEAD_DIM = 256
NA_WIDTH = NA_HEADS * HEAD_DIM
SW_WIDTH = SW_HEADS * HEAD_DIM
SW_KV_WIDTH = SW_KV_HEADS * HEAD_DIM
MEM_WIDTH = MEM_HEADS * MEM_HEAD_DIM
IN_SPLITS = (NA_WIDTH, NA_WIDTH, NA_WIDTH, NA_WIDTH,
             SW_WIDTH, SW_KV_WIDTH, SW_KV_WIDTH, SW_WIDTH,
             MEM_WIDTH, MEM_WIDTH,
             D_MODEL, D_MODEL, D_MODEL)
IN_WIDTH = sum(IN_SPLITS)
(OFF_QA, OFF_KA, OFF_VA, OFF_ZA, OFF_QB, OFF_KB, OFF_VB, OFF_ZB,
 OFF_QM, OFF_ZM, OFF_GA, OFF_GB, OFF_GM) = [int(c) for c in np.cumsum((0,) + IN_SPLITS[:-1])]
RMS_EPS = 1e-6
NEG_INF = -1e30

LANES = 128
VMEM_LIMIT_BYTES = 56 * 1024 * 1024
NORM_ROWS = 256
MM_BM = 1024
MM_BN = 1024
MERGE_BM = 1024
MERGE_BN = 512
OUT_BM = 512
OUT_BN = 512
NA_QROWS = 4
NA_KROWS = 12
MEM_QBLOCK = 512

_NT_DIMS = (((1,), (1,)), ((), ()))


def _params(*semantics):
    return pltpu.CompilerParams(dimension_semantics=semantics, vmem_limit_bytes=VMEM_LIMIT_BYTES)


def _sigmoid(x):
    return 1.0 / (1.0 + jnp.exp(-x))


def _rmsnorm_kernel(x_ref, g_ref, o_ref):
    x = x_ref[...]
    ms = jnp.mean(x * x, axis=-1, keepdims=True)
    o_ref[...] = ((x * lax.rsqrt(ms + RMS_EPS)) * g_ref[...]).astype(o_ref.dtype)


def _rmsnorm(x2d, gain):
    rows, d = x2d.shape
    return pl.pallas_call(
        _rmsnorm_kernel,
        grid=(rows // NORM_ROWS,),
        in_specs=[pl.BlockSpec((NORM_ROWS, d), lambda i: (i, 0)),
                  pl.BlockSpec((1, d), lambda i: (0, 0))],
        out_specs=pl.BlockSpec((NORM_ROWS, d), lambda i: (i, 0)),
        out_shape=jax.ShapeDtypeStruct((rows, d), jnp.bfloat16),
        compiler_params=_params("parallel"),
        name="rmsnorm",
    )(x2d, gain.reshape(1, d))


def _matmul_kernel(x_ref, w_ref, s_ref, o_ref):
    acc = jnp.dot(x_ref[...], w_ref[...], preferred_element_type=jnp.float32)
    o_ref[...] = (acc * s_ref[...]).astype(o_ref.dtype)


def _matmul(x2d, w, col_scale):
    m, k = x2d.shape
    n = w.shape[1]
    bm = min(MM_BM, m)
    bn = min(MM_BN, n)
    return pl.pallas_call(
        _matmul_kernel,
        grid=(m // bm, n // bn),
        in_specs=[pl.BlockSpec((bm, k), lambda i, j: (i, 0)),
                  pl.BlockSpec((k, bn), lambda i, j: (0, j)),
                  pl.BlockSpec((1, bn), lambda i, j: (0, j))],
        out_specs=pl.BlockSpec((bm, bn), lambda i, j: (i, j)),
        out_shape=jax.ShapeDtypeStruct((m, n), jnp.bfloat16),
        compiler_params=_params("parallel", "arbitrary"),
        name="proj_matmul",
    )(x2d, w, col_scale)


def _na_kernel(q_ref, k_ref, v_ref, z_ref, bias_ref, o_ref):
    seq = q_ref.shape[0]
    qb = NA_QROWS * GRID_W
    kb = NA_KROWS * GRID_W
    nblk = seq // qb
    grid_rows = seq // GRID_W

    for i in range(nblk):
        case = 0 if i == 0 else (2 if i == nblk - 1 else 1)
        q0 = i * qb
        k0 = min(max(NA_QROWS * i - NA_ROWS // 2, 0), grid_rows - NA_KROWS) * GRID_W
        q = q_ref[q0:q0 + qb, :]
        k = k_ref[k0:k0 + kb, :]
        v = v_ref[k0:k0 + kb, :]
        s = lax.dot_general(q, k, _NT_DIMS, preferred_element_type=jnp.float32) + bias_ref[case]
        mx = jnp.max(s, axis=-1, keepdims=True)
        e = jnp.exp(s - mx)
        den = jnp.sum(e, axis=-1, keepdims=True)
        o = jnp.dot(e.astype(v.dtype), v, preferred_element_type=jnp.float32) / den
        z = z_ref[q0:q0 + qb, :].astype(jnp.float32)
        o_ref[q0:q0 + qb, :] = (o * (z * _sigmoid(z))).astype(o_ref.dtype)


def _na_attention(proj, bias):
    bsz, seq, _ = proj.shape
    qb = NA_QROWS * GRID_W
    kb = NA_KROWS * GRID_W

    def col(off):
        return pl.BlockSpec((None, seq, HEAD_DIM), lambda h, b: (b, 0, off // HEAD_DIM + h))

    return pl.pallas_call(
        _na_kernel,
        grid=(NA_HEADS, bsz),
        in_specs=[col(OFF_QA), col(OFF_KA), col(OFF_VA), col(OFF_ZA),
                  pl.BlockSpec((None, 3, qb, kb), lambda h, b: (h, 0, 0, 0))],
        out_specs=pl.BlockSpec((None, seq, HEAD_DIM), lambda h, b: (b, 0, h)),
        out_shape=jax.ShapeDtypeStruct((bsz, seq, NA_WIDTH), jnp.bfloat16),
        compiler_params=_params("parallel", "arbitrary"),
        name="na_attention",
    )(proj, proj, proj, proj, bias)


def _na_bias_tables(rpb, seq):
    grid_rows = seq // GRID_W
    nblk = grid_rows // NA_QROWS
    win_rows = min(NA_ROWS, grid_rows)
    qc = np.arange(GRID_W)
    cstart = np.clip(qc - NA_COLS // 2, 0, GRID_W - NA_COLS)
    kc = np.arange(GRID_W)
    col_ok = (kc[None, :] >= cstart[:, None]) & (kc[None, :] < cstart[:, None] + NA_COLS)
    dc = np.clip(kc[None, :] - qc[:, None], -(NA_COLS - 1), NA_COLS - 1) + NA_COLS - 1
    row_ok, dr = [], []
    for i in (0, 1, nblk - 1):
        qr = NA_QROWS * i + np.arange(NA_QROWS)
        kr = np.clip(NA_QROWS * i - NA_ROWS // 2, 0, grid_rows - NA_KROWS) + np.arange(NA_KROWS)
        rs = np.clip(qr - win_rows // 2, 0, grid_rows - win_rows)
        ok = (kr[None, :] >= rs[:, None]) & (kr[None, :] < rs[:, None] + win_rows)
        row_ok.append(ok)
        dr.append(np.clip(kr[None, :] - qr[:, None] + NA_ROWS - 1, 0, 2 * NA_ROWS - 2))
    row_ok = np.stack(row_ok)
    dr = np.stack(dr)
    nd, ne = 2 * NA_ROWS - 1, 2 * NA_COLS - 1
    row_hot = (dr[..., None] == np.arange(nd)).astype(np.float32)
    col_hot = (dc[..., None] == np.arange(ne)).astype(np.float32)
    pair = LANES // GRID_W
    pair_hot = np.zeros((pair, ne, GRID_W, pair, GRID_W), np.float32)
    for b in range(pair):
        pair_hot[b, :, :, b, :] = np.transpose(col_hot, (2, 0, 1))
    pair_hot = pair_hot.reshape(pair * ne, GRID_W, LANES)
    mask = row_ok[:, :, None, :, None] & col_ok[None, None, :, None, :]
    mask_add = np.where(mask, 0.0, NEG_INF).astype(np.float32).reshape(
        3, NA_QROWS * GRID_W, NA_KROWS * GRID_W)
    nl, nh = rpb.shape[0], rpb.shape[1]
    g = jnp.einsum("sabd,lhde->lhsabe", jnp.asarray(row_hot), rpb.astype(jnp.float32),
                   precision=lax.Precision.HIGHEST)
    g = g.reshape(nl, nh, 3, NA_QROWS, NA_KROWS // pair, pair * ne)
    t = jnp.einsum("lhsapk,kxz->lhsaxpz", g, jnp.asarray(pair_hot), precision=lax.Precision.HIGHEST)
    t = t.reshape(nl, nh, 3, NA_QROWS * GRID_W, NA_KROWS * GRID_W)
    return t + jnp.asarray(mask_add)[None, None]


def _sw_kernel(q_ref, k_ref, v_ref, z0_ref, z1_ref, z2_ref, bias_ref, sink_ref, o_ref):
    z_refs = (z0_ref, z1_ref, z2_ref)
    seq = k_ref.shape[0]
    blk = SW_BLOCK
    nblk = seq // blk
    sk = sink_ref[...]

    for i in range(nblk):
        r0 = i * blk
        lo = max(i - 1, 0) * blk
        hi = min(i + 2, nblk) * blk
        c0 = lo - (i - 1) * blk
        q = q_ref[r0:r0 + blk, :]
        qs = jnp.concatenate([q[:, g * HEAD_DIM:(g + 1) * HEAD_DIM] for g in range(SW_GROUP)], axis=0)
        kw = k_ref[lo:hi, :]
        vw = v_ref[lo:hi, :]
        s = (lax.dot_general(qs, kw, _NT_DIMS, preferred_element_type=jnp.float32)
             + bias_ref[:, c0:c0 + hi - lo])
        mx = jnp.maximum(jnp.max(s, axis=-1, keepdims=True), sk)
        e = jnp.exp(s - mx)
        den = jnp.sum(e, axis=-1, keepdims=True) + jnp.exp(sk - mx)
        o = jnp.dot(e.astype(vw.dtype), vw, preferred_element_type=jnp.float32) / den
        og = jnp.concatenate([o[g * blk:(g + 1) * blk, :] for g in range(SW_GROUP)], axis=1)
        z = jnp.concatenate([r[r0:r0 + blk, :] for r in z_refs], axis=1).astype(jnp.float32)
        o_ref[r0:r0 + blk, :] = (og * (z * _sigmoid(z))).astype(o_ref.dtype)


def _sw_attention(proj, bias, sink_col):
    bsz, seq, _ = proj.shape
    gw = SW_GROUP * HEAD_DIM

    def wide(off):
        return pl.BlockSpec((None, seq, gw), lambda k, b: (b, 0, off // gw + k))

    def narrow(off):
        return pl.BlockSpec((None, seq, HEAD_DIM), lambda k, b: (b, 0, off // HEAD_DIM + k))

    def gate(g):
        return pl.BlockSpec((None, seq, HEAD_DIM),
                            lambda k, b: (b, 0, OFF_ZB // HEAD_DIM + k * SW_GROUP + g))

    assert OFF_QB % gw == 0 and SW_GROUP == 3
    return pl.pallas_call(
        _sw_kernel,
        grid=(SW_KV_HEADS, bsz),
        in_specs=[wide(OFF_QB), narrow(OFF_KB), narrow(OFF_VB), gate(0), gate(1), gate(2),
                  pl.BlockSpec((SW_GROUP * SW_BLOCK, 3 * SW_BLOCK), lambda k, b: (k, 0)),
                  pl.BlockSpec((SW_GROUP * SW_BLOCK, 1), lambda k, b: (k, 0))],
        out_specs=pl.BlockSpec((None, seq, gw), lambda k, b: (b, 0, k)),
        out_shape=jax.ShapeDtypeStruct((bsz, seq, SW_WIDTH), jnp.bfloat16),
        compiler_params=_params("parallel", "arbitrary"),
        name="sw_attention",
    )(proj, proj, proj, proj, proj, proj, bias, sink_col)


def _t5_bucket_index(rel):
    nb = T5_BUCKETS // 2
    max_exact = nb // 2
    ret = (rel > 0).astype(np.int32) * nb
    n = np.abs(rel)
    large = max_exact + (np.log(np.maximum(n, 1) / max_exact)
                         / np.log(T5_MAX_DIST / max_exact) * (nb - max_exact)).astype(np.int32)
    large = np.minimum(large, nb - 1)
    return (ret + np.where(n < max_exact, n, large)).astype(np.int32)


def _sw_bias_table(t5_bias):
    rel = np.arange(3 * SW_BLOCK)[None, :] - SW_BLOCK - np.arange(SW_BLOCK)[:, None]
    band = np.abs(rel) <= SW_WINDOW
    hot = (_t5_bucket_index(rel)[..., None] == np.arange(T5_BUCKETS)).astype(np.float32)
    t = jnp.einsum("qkb,bh->hqk", jnp.asarray(hot), t5_bias.astype(jnp.float32),
                   precision=lax.Precision.HIGHEST)
    t = jnp.where(jnp.asarray(band)[None], t, NEG_INF)
    return t.reshape(SW_HEADS * SW_BLOCK, 3 * SW_BLOCK)


def _mem_kernel(q_ref, z_ref, k_ref, v_ref, o_ref):
    seq = q_ref.shape[0]
    k = k_ref[...]
    v = v_ref[...]

    for i in range(seq // MEM_QBLOCK):
        r0 = i * MEM_QBLOCK
        q = q_ref[r0:r0 + MEM_QBLOCK, :]
        s = lax.dot_general(q, k, _NT_DIMS, preferred_element_type=jnp.float32)
        mx = jnp.max(s, axis=-1, keepdims=True)
        e = jnp.exp(s - mx)
        den = jnp.sum(e, axis=-1, keepdims=True)
        o = jnp.dot(e.astype(v.dtype), v, preferred_element_type=jnp.float32) / den
        z = z_ref[r0:r0 + MEM_QBLOCK, :].astype(jnp.float32)
        o_ref[r0:r0 + MEM_QBLOCK, :] = (o * (z * _sigmoid(z))).astype(o_ref.dtype)


def _mem_attention(proj, memkv):
    bsz, seq, _ = proj.shape
    mtok = memkv.shape[1]

    def qcol(off):
        return pl.BlockSpec((None, seq, MEM_HEAD_DIM), lambda h, b: (b, 0, off // MEM_HEAD_DIM + h))

    def kvcol(off):
        return pl.BlockSpec((None, mtok, MEM_HEAD_DIM), lambda h, b: (b, 0, off // MEM_HEAD_DIM + h))

    assert OFF_QM % MEM_HEAD_DIM == 0 and OFF_ZM % MEM_HEAD_DIM == 0
    return pl.pallas_call(
        _mem_kernel,
        grid=(MEM_HEADS, bsz),
        in_specs=[qcol(OFF_QM), qcol(OFF_ZM), kvcol(0), kvcol(MEM_WIDTH)],
        out_specs=pl.BlockSpec((None, seq, MEM_HEAD_DIM), lambda h, b: (b, 0, h)),
        out_shape=jax.ShapeDtypeStruct((bsz, seq, MEM_WIDTH), jnp.bfloat16),
        compiler_params=_params("parallel", "arbitrary"),
        name="mem_attention",
    )(proj, proj, memkv, memkv)


def _merge_kernel(a_ref, b_ref, m_ref, wa_ref, wb_ref, wm_ref, ga_ref, gb_ref, gm_ref, o_ref):
    def branch(x_ref, w_ref, g_ref):
        y = jnp.dot(x_ref[...], w_ref[...], preferred_element_type=jnp.float32)
        return _sigmoid(g_ref[...].astype(jnp.float32)) * y

    acc = branch(a_ref, wa_ref, ga_ref)
    acc = acc + branch(b_ref, wb_ref, gb_ref)
    acc = acc + branch(m_ref, wm_ref, gm_ref)
    o_ref[...] = acc.astype(o_ref.dtype)


def _merge(a2d, b2d, m2d, wa, wb, wm, proj2d):
    rows = a2d.shape[0]
    bm = min(MERGE_BM, rows)
    bn = MERGE_BN

    def act(width):
        return pl.BlockSpec((bm, width), lambda i, j: (i, 0))

    def wgt(width):
        return pl.BlockSpec((width, bn), lambda i, j: (0, j))

    def gate(off):
        return pl.BlockSpec((bm, bn), lambda i, j: (i, off // bn + j))

    assert OFF_GA % bn == 0 and OFF_GB % bn == 0 and OFF_GM % bn == 0 and rows % bm == 0
    return pl.pallas_call(
        _merge_kernel,
        grid=(rows // bm, D_MODEL // bn),
        in_specs=[act(NA_WIDTH), act(SW_WIDTH), act(MEM_WIDTH),
                  wgt(NA_WIDTH), wgt(SW_WIDTH), wgt(MEM_WIDTH),
                  gate(OFF_GA), gate(OFF_GB), gate(OFF_GM)],
        out_specs=pl.BlockSpec((bm, bn), lambda i, j: (i, j)),
        out_shape=jax.ShapeDtypeStruct((rows, D_MODEL), jnp.bfloat16),
        compiler_params=_params("parallel", "arbitrary"),
        name="branch_merge",
    )(a2d, b2d, m2d, wa, wb, wm, proj2d, proj2d, proj2d)


def _outproj_kernel(mg_ref, w_ref, x_ref, g_ref, o_ref):
    j = pl.program_id(1)
    bn = w_ref.shape[1]
    col = pl.multiple_of(j * bn, bn)
    o_ref[:, pl.ds(col, bn)] = jnp.dot(mg_ref[...], w_ref[...], preferred_element_type=jnp.float32)

    @pl.when(j == pl.num_programs(1) - 1)
    def _():
        y = o_ref[...]
        ms = jnp.mean(y * y, axis=-1, keepdims=True)
        o_ref[...] = x_ref[...] + (y * lax.rsqrt(ms + RMS_EPS)) * g_ref[...]


def _outproj(merged2d, w, x2d, gain):
    rows, d = x2d.shape
    bm = min(OUT_BM, rows)
    bn = OUT_BN
    return pl.pallas_call(
        _outproj_kernel,
        grid=(rows // bm, d // bn),
        in_specs=[pl.BlockSpec((bm, d), lambda i, j: (i, 0)),
                  pl.BlockSpec((d, bn), lambda i, j: (0, j)),
                  pl.BlockSpec((bm, d), lambda i, j: (i, 0)),
                  pl.BlockSpec((1, d), lambda i, j: (0, 0))],
        out_specs=pl.BlockSpec((bm, d), lambda i, j: (i, 0)),
        out_shape=jax.ShapeDtypeStruct((rows, d), jnp.float32),
        compiler_params=_params("parallel", "arbitrary"),
        name="out_proj_norm",
    )(merged2d, w, x2d, gain.reshape(1, d))


def _in_proj_scale():
    s = np.ones((1, IN_WIDTH), np.float32)
    s[0, OFF_QA:OFF_QA + NA_WIDTH] = HEAD_DIM ** -0.5
    s[0, OFF_QB:OFF_QB + SW_WIDTH] = HEAD_DIM ** -0.5
    s[0, OFF_QM:OFF_QM + MEM_WIDTH] = MEM_HEAD_DIM ** -0.5
    return s


def _trunk(x, mem, p):
    bsz, seq, d = x.shape
    mtok = mem.shape[1]
    x2d = x.reshape(bsz * seq, d)
    mem2d = mem.reshape(bsz * mtok, d)
    for l in range(DEPTH):
        h = _rmsnorm(x2d, p["pre_norm"][l])
        proj2d = _matmul(h, p["w_in"][l], p["in_scale"])
        proj = proj2d.reshape(bsz, seq, IN_WIDTH)
        mem_h = _rmsnorm(mem2d, p["mem_norm"][l])
        memkv = _matmul(mem_h, p["w_mem_kv"][l], p["kv_scale"]).reshape(bsz, mtok, 2 * MEM_WIDTH)
        a = _na_attention(proj, p["na_bias"][l])
        b = _sw_attention(proj, p["sw_bias"], p["sink_col"][l])
        m = _mem_attention(proj, memkv)
        merged = _merge(a.reshape(bsz * seq, NA_WIDTH), b.reshape(bsz * seq, SW_WIDTH),
                        m.reshape(bsz * seq, MEM_WIDTH), p["w_branch_a"][l], p["w_branch_b"][l],
                        p["w_branch_m"][l], proj2d)
        x2d = _outproj(merged, p["w_out"][l], x2d, p["post_norm"][l])
    return x2d.reshape(bsz, seq, d)


def kernel(x_prompt, x_sample, mem_prompt, mem_sample, pre_norm, post_norm, mem_norm, w_in, w_mem_kv,
           w_branch_a, w_branch_b, w_branch_m, w_out, na_rpb, attn_sink, t5_bias):
    bf16 = jnp.bfloat16
    seq = x_prompt.shape[1]
    assert x_sample.shape[1] == seq and seq % (NA_QROWS * GRID_W) == 0 and seq % SW_BLOCK == 0
    assert seq // GRID_W >= NA_KROWS and seq % MEM_QBLOCK == 0 and seq // SW_BLOCK >= 2
    assert LANES % GRID_W == 0 and NA_KROWS % (LANES // GRID_W) == 0
    sink_col = jnp.repeat(attn_sink.astype(jnp.float32), SW_BLOCK, axis=1)[..., None]
    p = {
        "pre_norm": pre_norm, "post_norm": post_norm, "mem_norm": mem_norm,
        "w_in": w_in.astype(bf16), "w_mem_kv": w_mem_kv.astype(bf16),
        "w_branch_a": w_branch_a.astype(bf16), "w_branch_b": w_branch_b.astype(bf16),
        "w_branch_m": w_branch_m.astype(bf16), "w_out": w_out.astype(bf16),
        "in_scale": jnp.asarray(_in_proj_scale()),
        "kv_scale": jnp.ones((1, 2 * MEM_WIDTH), jnp.float32),
        "na_bias": _na_bias_tables(na_rpb, seq),
        "sw_bias": _sw_bias_table(t5_bias),
        "sink_col": sink_col,
    }
    y_prompt = _trunk(x_prompt, mem_prompt, p)
    y_sample = _trunk(x_sample, mem_sample, p)
    return (y_prompt, y_sample)
```

```python
import functools

import numpy as np
import jax
import jax.numpy as jnp
from jax import lax
from jax.experimental import pallas as pl
from jax.experimental.pallas import tpu as pltpu

D_MODEL = 4096
DEPTH = 2
GRID_W = 64
HEAD_DIM = 128
NA_HEADS = 12
NA_ROWS = 8
NA_COLS = 16
SW_HEADS = 12
SW_KV_HEADS = 4
SW_GROUP = SW_HEADS // SW_KV_HEADS
SW_WINDOW = 128
SW_BLOCK = 128
T5_BUCKETS = 32
T5_MAX_DIST = 128
MEM_HEADS = 4
MEM_HEAD_DIM = 256
NA_WIDTH = NA_HEADS * HEAD_DIM
SW_WIDTH = SW_HEADS * HEAD_DIM
SW_KV_WIDTH = SW_KV_HEADS * HEAD_DIM
MEM_WIDTH = MEM_HEADS * MEM_HEAD_DIM
IN_SPLITS = (NA_WIDTH, NA_WIDTH, NA_WIDTH, NA_WIDTH,
             SW_WIDTH, SW_KV_WIDTH, SW_KV_WIDTH, SW_WIDTH,
             MEM_WIDTH, MEM_WIDTH,
             D_MODEL, D_MODEL, D_MODEL)
IN_WIDTH = sum(IN_SPLITS)
(OFF_QA, OFF_KA, OFF_VA, OFF_ZA, OFF_QB, OFF_KB, OFF_VB, OFF_ZB,
 OFF_QM, OFF_ZM, OFF_GA, OFF_GB, OFF_GM) = [int(c) for c in np.cumsum((0,) + IN_SPLITS[:-1])]
RMS_EPS = 1e-6
NEG_INF = -1e30
LOG2E = float(np.log2(np.e))

LANES = 128
VMEM_LIMIT_BYTES = 56 * 1024 * 1024
NORM_ROWS = 256
MM_BM = 1024
MM_BN = 1024
MERGE_BM = 1024
MERGE_BN = 512
OUT_BM = 1024
OUT_BN = 512
NA_QROWS = 4
NA_KROWS = 12
MEM_QBLOCK = 512

_NT_DIMS = (((1,), (1,)), ((), ()))


def _params(*semantics):
    return pltpu.CompilerParams(dimension_semantics=semantics, vmem_limit_bytes=VMEM_LIMIT_BYTES)


def _weighted_sum_and_total(e, v):
    hd = v.shape[1]
    r = jnp.dot(e, jnp.concatenate([v, jnp.ones_like(v)], axis=1), preferred_element_type=jnp.float32)
    return r[:, :hd], r[:, hd:]


def _sigmoid(x):
    return 0.5 * jnp.tanh(0.5 * x) + 0.5


def _rmsnorm_kernel(x_ref, g_ref, o_ref):
    x = x_ref[...]
    ms = jnp.mean(x * x, axis=-1, keepdims=True)
    o_ref[...] = ((x * lax.rsqrt(ms + RMS_EPS)) * g_ref[...]).astype(o_ref.dtype)


def _rmsnorm(x2d, gain):
    rows, d = x2d.shape
    return pl.pallas_call(
        _rmsnorm_kernel,
        grid=(rows // NORM_ROWS,),
        in_specs=[pl.BlockSpec((NORM_ROWS, d), lambda i: (i, 0)),
                  pl.BlockSpec((1, d), lambda i: (0, 0))],
        out_specs=pl.BlockSpec((NORM_ROWS, d), lambda i: (i, 0)),
        out_shape=jax.ShapeDtypeStruct((rows, d), jnp.bfloat16),
        compiler_params=_params("parallel"),
        name="rmsnorm",
    )(x2d, gain.reshape(1, d))


def _matmul_kernel(x_ref, w_ref, s_ref, o_ref):
    acc = jnp.dot(x_ref[...], w_ref[...], preferred_element_type=jnp.float32)
    o_ref[...] = (acc * s_ref[...]).astype(o_ref.dtype)


def _matmul(x2d, w, col_scale):
    m, k = x2d.shape
    n = w.shape[1]
    bm = min(MM_BM, m)
    bn = min(MM_BN, n)
    return pl.pallas_call(
        _matmul_kernel,
        grid=(m // bm, n // bn),
        in_specs=[pl.BlockSpec((bm, k), lambda i, j: (i, 0)),
                  pl.BlockSpec((k, bn), lambda i, j: (0, j)),
                  pl.BlockSpec((1, bn), lambda i, j: (0, j))],
        out_specs=pl.BlockSpec((bm, bn), lambda i, j: (i, j)),
        out_shape=jax.ShapeDtypeStruct((m, n), jnp.bfloat16),
        compiler_params=_params("parallel", "arbitrary"),
        name="proj_matmul",
    )(x2d, w, col_scale)


def _na_kernel(q_ref, k_ref, v_ref, z_ref, bias_ref, o_ref):
    seq = q_ref.shape[0]
    qb = NA_QROWS * GRID_W
    kb = NA_KROWS * GRID_W
    nblk = seq // qb
    grid_rows = seq // GRID_W

    for i in range(nblk):
        case = 0 if i == 0 else (2 if i == nblk - 1 else 1)
        q0 = i * qb
        k0 = min(max(NA_QROWS * i - NA_ROWS // 2, 0), grid_rows - NA_KROWS) * GRID_W
        q = q_ref[q0:q0 + qb, :]
        k = k_ref[k0:k0 + kb, :]
        v = v_ref[k0:k0 + kb, :]
        s = lax.dot_general(q, k, _NT_DIMS, preferred_element_type=jnp.float32) + bias_ref[case]
        mx = jnp.max(s, axis=-1, keepdims=True)
        num, tot = _weighted_sum_and_total(jnp.exp2(s - mx).astype(v.dtype), v)
        o = num / tot
        z = z_ref[q0:q0 + qb, :].astype(jnp.float32)
        o_ref[q0:q0 + qb, :] = (o * (z * _sigmoid(z))).astype(o_ref.dtype)


def _na_attention(proj, bias):
    bsz, seq, _ = proj.shape
    qb = NA_QROWS * GRID_W
    kb = NA_KROWS * GRID_W

    def col(off):
        return pl.BlockSpec((None, seq, HEAD_DIM), lambda h, b: (b, 0, off // HEAD_DIM + h))

    return pl.pallas_call(
        _na_kernel,
        grid=(NA_HEADS, bsz),
        in_specs=[col(OFF_QA), col(OFF_KA), col(OFF_VA), col(OFF_ZA),
                  pl.BlockSpec((None, 3, qb, kb), lambda h, b: (h, 0, 0, 0))],
        out_specs=pl.BlockSpec((None, seq, HEAD_DIM), lambda h, b: (b, 0, h)),
        out_shape=jax.ShapeDtypeStruct((bsz, seq, NA_WIDTH), jnp.bfloat16),
        compiler_params=_params("parallel", "arbitrary"),
        name="na_attention",
    )(proj, proj, proj, proj, bias)


def _na_bias_tables(rpb, seq):
    grid_rows = seq // GRID_W
    nblk = grid_rows // NA_QROWS
    win_rows = min(NA_ROWS, grid_rows)
    qc = np.arange(GRID_W)
    cstart = np.clip(qc - NA_COLS // 2, 0, GRID_W - NA_COLS)
    kc = np.arange(GRID_W)
    col_ok = (kc[None, :] >= cstart[:, None]) & (kc[None, :] < cstart[:, None] + NA_COLS)
    dc = np.clip(kc[None, :] - qc[:, None], -(NA_COLS - 1), NA_COLS - 1) + NA_COLS - 1
    row_ok, dr = [], []
    for i in (0, 1, nblk - 1):
        qr = NA_QROWS * i + np.arange(NA_QROWS)
        kr = np.clip(NA_QROWS * i - NA_ROWS // 2, 0, grid_rows - NA_KROWS) + np.arange(NA_KROWS)
        rs = np.clip(qr - win_rows // 2, 0, grid_rows - win_rows)
        ok = (kr[None, :] >= rs[:, None]) & (kr[None, :] < rs[:, None] + win_rows)
        row_ok.append(ok)
        dr.append(np.clip(kr[None, :] - qr[:, None] + NA_ROWS - 1, 0, 2 * NA_ROWS - 2))
    row_ok = np.stack(row_ok)
    dr = np.stack(dr)
    nd, ne = 2 * NA_ROWS - 1, 2 * NA_COLS - 1
    row_hot = (dr[..., None] == np.arange(nd)).astype(np.float32)
    col_hot = (dc[..., None] == np.arange(ne)).astype(np.float32)
    pair = LANES // GRID_W
    pair_hot = np.zeros((pair, ne, GRID_W, pair, GRID_W), np.float32)
    for b in range(pair):
        pair_hot[b, :, :, b, :] = np.transpose(col_hot, (2, 0, 1))
    pair_hot = pair_hot.reshape(pair * ne, GRID_W, LANES)
    mask = row_ok[:, :, None, :, None] & col_ok[None, None, :, None, :]
    mask_add = np.where(mask, 0.0, NEG_INF).astype(np.float32).reshape(
        3, NA_QROWS * GRID_W, NA_KROWS * GRID_W)
    nl, nh = rpb.shape[0], rpb.shape[1]
    g = jnp.einsum("sabd,lhde->lhsabe", jnp.asarray(row_hot), rpb.astype(jnp.float32),
                   precision=lax.Precision.HIGHEST)
    g = g.reshape(nl, nh, 3, NA_QROWS, NA_KROWS // pair, pair * ne)
    t = jnp.einsum("lhsapk,kxz->lhsaxpz", g, jnp.asarray(pair_hot), precision=lax.Precision.HIGHEST)
    t = t.reshape(nl, nh, 3, NA_QROWS * GRID_W, NA_KROWS * GRID_W)
    return t * LOG2E + jnp.asarray(mask_add)[None, None]


def _sw_kernel(q_ref, k_ref, v_ref, z0_ref, z1_ref, z2_ref, bias_ref, sink_ref, o_ref):
    z_refs = (z0_ref, z1_ref, z2_ref)
    seq = k_ref.shape[0]
    blk = SW_BLOCK
    nblk = seq // blk
    sk = sink_ref[...]

    for i in range(nblk):
        r0 = i * blk
        lo = max(i - 1, 0) * blk
        hi = min(i + 2, nblk) * blk
        c0 = lo - (i - 1) * blk
        q = q_ref[r0:r0 + blk, :]
        qs = jnp.concatenate([q[:, g * HEAD_DIM:(g + 1) * HEAD_DIM] for g in range(SW_GROUP)], axis=0)
        kw = k_ref[lo:hi, :]
        vw = v_ref[lo:hi, :]
        s = (lax.dot_general(qs, kw, _NT_DIMS, preferred_element_type=jnp.float32)
             + bias_ref[:, c0:c0 + hi - lo])
        mx = jnp.maximum(jnp.max(s, axis=-1, keepdims=True), sk)
        num, tot = _weighted_sum_and_total(jnp.exp2(s - mx).astype(vw.dtype), vw)
        o = num / (tot + jnp.exp2(sk - mx))
        og =jnp.concatenate([o[g * blk:(g + 1) * blk, :] for g in range(SW_GROUP)], axis=1)
        z = jnp.concatenate([r[r0:r0 + blk, :] for r in z_refs], axis=1).astype(jnp.float32)
        o_ref[r0:r0 + blk, :] = (og * (z * _sigmoid(z))).astype(o_ref.dtype)


def _sw_attention(proj, bias, sink_col):
    bsz, seq, _ = proj.shape
    gw = SW_GROUP * HEAD_DIM

    def wide(off):
        return pl.BlockSpec((None, seq, gw), lambda k, b: (b, 0, off // gw + k))

    def narrow(off):
        return pl.BlockSpec((None, seq, HEAD_DIM), lambda k, b: (b, 0, off // HEAD_DIM + k))

    def gate(g):
        return pl.BlockSpec((None, seq, HEAD_DIM),
                            lambda k, b: (b, 0, OFF_ZB // HEAD_DIM + k * SW_GROUP + g))

    assert OFF_QB % gw == 0 and SW_GROUP == 3
    return pl.pallas_call(
        _sw_kernel,
        grid=(SW_KV_HEADS, bsz),
        in_specs=[wide(OFF_QB), narrow(OFF_KB), narrow(OFF_VB), gate(0), gate(1), gate(2),
                  pl.BlockSpec((SW_GROUP * SW_BLOCK, 3 * SW_BLOCK), lambda k, b: (k, 0)),
                  pl.BlockSpec((SW_GROUP * SW_BLOCK, 1), lambda k, b: (k, 0))],
        out_specs=pl.BlockSpec((None, seq, gw), lambda k, b: (b, 0, k)),
        out_shape=jax.ShapeDtypeStruct((bsz, seq, SW_WIDTH), jnp.bfloat16),
        compiler_params=_params("parallel", "arbitrary"),
        name="sw_attention",
    )(proj, proj, proj, proj, proj, proj, bias, sink_col)


def _t5_bucket_index(rel):
    nb = T5_BUCKETS // 2
    max_exact = nb // 2
    ret = (rel > 0).astype(np.int32) * nb
    n = np.abs(rel)
    large = max_exact + (np.log(np.maximum(n, 1) / max_exact)
                         / np.log(T5_MAX_DIST / max_exact) * (nb - max_exact)).astype(np.int32)
    large = np.minimum(large, nb - 1)
    return (ret + np.where(n < max_exact, n, large)).astype(np.int32)


def _sw_bias_table(t5_bias):
    rel = np.arange(3 * SW_BLOCK)[None, :] - SW_BLOCK - np.arange(SW_BLOCK)[:, None]
    band = np.abs(rel) <= SW_WINDOW
    hot = (_t5_bucket_index(rel)[..., None] == np.arange(T5_BUCKETS)).astype(np.float32)
    t = jnp.einsum("qkb,bh->hqk", jnp.asarray(hot), t5_bias.astype(jnp.float32),
                   precision=lax.Precision.HIGHEST)
    t = jnp.where(jnp.asarray(band)[None], t * LOG2E, NEG_INF)
    return t.reshape(SW_HEADS * SW_BLOCK, 3 * SW_BLOCK)


def _mem_kernel(q_ref, z_ref, k_ref, v_ref, o_ref):
    seq = q_ref.shape[0]
    k = k_ref[...]
    v = v_ref[...]

    for i in range(seq // MEM_QBLOCK):
        r0 = i * MEM_QBLOCK
        q = q_ref[r0:r0 + MEM_QBLOCK, :]
        s = lax.dot_general(q, k, _NT_DIMS, preferred_element_type=jnp.float32)
        mx = jnp.max(s, axis=-1, keepdims=True)
        e = jnp.exp2(s - mx)
        den = jnp.sum(e, axis=-1, keepdims=True)
        o = jnp.dot(e.astype(v.dtype), v, preferred_element_type=jnp.float32) / den
        z = z_ref[r0:r0 + MEM_QBLOCK, :].astype(jnp.float32)
        o_ref[r0:r0 + MEM_QBLOCK, :] = (o * (z * _sigmoid(z))).astype(o_ref.dtype)


def _mem_attention(proj, memkv):
    bsz, seq, _ = proj.shape
    mtok = memkv.shape[1]

    def qcol(off):
        return pl.BlockSpec((None, seq, MEM_HEAD_DIM), lambda h, b: (b, 0, off // MEM_HEAD_DIM + h))

    def kvcol(off):
        return pl.BlockSpec((None, mtok, MEM_HEAD_DIM), lambda h, b: (b, 0, off // MEM_HEAD_DIM + h))

    assert OFF_QM % MEM_HEAD_DIM == 0 and OFF_ZM % MEM_HEAD_DIM == 0
    return pl.pallas_call(
        _mem_kernel,
        grid=(MEM_HEADS, bsz),
        in_specs=[qcol(OFF_QM), qcol(OFF_ZM), kvcol(0), kvcol(MEM_WIDTH)],
        out_specs=pl.BlockSpec((None, seq, MEM_HEAD_DIM), lambda h, b: (b, 0, h)),
        out_shape=jax.ShapeDtypeStruct((bsz, seq, MEM_WIDTH), jnp.bfloat16),
        compiler_params=_params("parallel", "arbitrary"),
        name="mem_attention",
    )(proj, proj, memkv, memkv)


def _merge_kernel(a_ref, b_ref, m_ref, wa_ref, wb_ref, wm_ref, ga_ref, gb_ref, gm_ref, o_ref):
    def branch(x_ref, w_ref, g_ref):
        y = jnp.dot(x_ref[...], w_ref[...], preferred_element_type=jnp.float32)
        return _sigmoid(g_ref[...].astype(jnp.float32)) * y

    acc = branch(a_ref, wa_ref, ga_ref)
    acc = acc + branch(b_ref, wb_ref, gb_ref)
    acc = acc + branch(m_ref, wm_ref, gm_ref)
    o_ref[...] = acc.astype(o_ref.dtype)


def _merge(a2d, b2d, m2d, wa, wb, wm, proj2d):
    rows = a2d.shape[0]
    bm = min(MERGE_BM, rows)
    bn = MERGE_BN

    def act(width):
        return pl.BlockSpec((bm, width), lambda i, j: (i, 0))

    def wgt(width):
        return pl.BlockSpec((width, bn), lambda i, j: (0, j))

    def gate(off):
        return pl.BlockSpec((bm, bn), lambda i, j: (i, off // bn + j))

    assert OFF_GA % bn == 0 and OFF_GB % bn == 0 and OFF_GM % bn == 0 and rows % bm == 0
    return pl.pallas_call(
        _merge_kernel,
        grid=(rows // bm, D_MODEL // bn),
        in_specs=[act(NA_WIDTH), act(SW_WIDTH), act(MEM_WIDTH),
                  wgt(NA_WIDTH), wgt(SW_WIDTH), wgt(MEM_WIDTH),
                  gate(OFF_GA), gate(OFF_GB), gate(OFF_GM)],
        out_specs=pl.BlockSpec((bm, bn), lambda i, j: (i, j)),
        out_shape=jax.ShapeDtypeStruct((rows, D_MODEL), jnp.bfloat16),
        compiler_params=_params("parallel", "arbitrary"),
        name="branch_merge",
    )(a2d, b2d, m2d, wa, wb, wm, proj2d, proj2d, proj2d)


def _outproj_kernel(mg_ref, w_ref, x_ref, g_ref, o_ref, y_ref, ssq_ref):
    i = pl.program_id(0)
    j = pl.program_id(1)
    n_tiles = pl.num_programs(0) - 1
    bn = w_ref.shape[1]
    d = y_ref.shape[1]
    col = pl.multiple_of(j * bn, bn)
    cur = lax.rem(i, 2)

    def finish():
        inv = lax.rsqrt(ssq_ref[1 - cur] * (1.0 / d) + RMS_EPS)
        o_ref[...] = x_ref[...] + (y_ref[:, pl.ds(col, bn)] * inv) * g_ref[...]

    def project():
        y = jnp.dot(mg_ref[...], w_ref[...], preferred_element_type=jnp.float32)
        y_ref[:, pl.ds(col, bn)] = y
        part = jnp.sum(y * y, axis=-1, keepdims=True)
        ssq_ref[cur] = part + jnp.where(j == 0, 0.0, ssq_ref[cur])

    @pl.when(i == 0)
    def _():
        @pl.when(j == 0)
        def _():
            ssq_ref[...] = jnp.zeros_like(ssq_ref)
        project()

    @pl.when((i > 0) & (i < n_tiles))
    def _():
        finish()
        project()

    @pl.when(i == n_tiles)
    def _():
        finish()


def _outproj(merged2d, w, x2d, gain):
    rows, d = x2d.shape
    bm = min(OUT_BM, rows)
    bn = OUT_BN
    n_tiles = rows // bm

    def finished_block(i, j):
        return (jnp.maximum(i - 1, 0), jnp.where(i == 0, 0, j))

    return pl.pallas_call(
        _outproj_kernel,
        grid=(n_tiles + 1, d // bn),
        in_specs=[pl.BlockSpec((bm, d), lambda i, j: (jnp.minimum(i, n_tiles - 1), 0)),
                  pl.BlockSpec((d, bn), lambda i, j: (0, j)),
                  pl.BlockSpec((bm, bn), finished_block),
                  pl.BlockSpec((1, bn), lambda i, j: (0, j))],
        out_specs=pl.BlockSpec((bm, bn), finished_block),
        out_shape=jax.ShapeDtypeStruct((rows, d), jnp.float32),
        scratch_shapes=[pltpu.VMEM((bm, d), jnp.float32),
                        pltpu.VMEM((2, bm, 1), jnp.float32)],
        compiler_params=_params("arbitrary", "arbitrary"),
        name="out_proj_norm",
    )(merged2d, w, x2d, gain.reshape(1, d))


def _in_proj_scale():
    s = np.ones((1, IN_WIDTH), np.float32)
    s[0, OFF_QA:OFF_QA + NA_WIDTH] = HEAD_DIM ** -0.5 * LOG2E
    s[0, OFF_QB:OFF_QB + SW_WIDTH] = HEAD_DIM ** -0.5 * LOG2E
    s[0, OFF_QM:OFF_QM + MEM_WIDTH] = MEM_HEAD_DIM ** -0.5 * LOG2E
    return s


def _trunk(x, mem, p):
    bsz, seq, d = x.shape
    mtok = mem.shape[1]
    x2d = x.reshape(bsz * seq, d)
    mem2d = mem.reshape(bsz * mtok, d)
    for l in range(DEPTH):
        h = _rmsnorm(x2d, p["pre_norm"][l])
        proj2d = _matmul(h, p["w_in"][l], p["in_scale"])
        proj = proj2d.reshape(bsz, seq, IN_WIDTH)
        mem_h = _rmsnorm(mem2d, p["mem_norm"][l])
        memkv = _matmul(mem_h, p["w_mem_kv"][l], p["kv_scale"]).reshape(bsz, mtok, 2 * MEM_WIDTH)
        a = _na_attention(proj, p["na_bias"][l])
        b = _sw_attention(proj, p["sw_bias"], p["sink_col"][l])
        m = _mem_attention(proj, memkv)
        merged = _merge(a.reshape(bsz * seq, NA_WIDTH), b.reshape(bsz * seq, SW_WIDTH),
                        m.reshape(bsz * seq, MEM_WIDTH), p["w_branch_a"][l], p["w_branch_b"][l],
                        p["w_branch_m"][l], proj2d)
        x2d = _outproj(merged, p["w_out"][l], x2d, p["post_norm"][l])
    return x2d.reshape(bsz, seq, d)


def kernel(x_prompt, x_sample, mem_prompt, mem_sample, pre_norm, post_norm, mem_norm, w_in, w_mem_kv,
           w_branch_a, w_branch_b, w_branch_m, w_out, na_rpb, attn_sink, t5_bias):
    bf16 = jnp.bfloat16
    seq = x_prompt.shape[1]
    assert x_sample.shape[1] == seq and seq % (NA_QROWS * GRID_W) == 0 and seq % SW_BLOCK == 0
    assert seq // GRID_W >= NA_KROWS and seq % MEM_QBLOCK == 0 and seq // SW_BLOCK >= 2
    assert LANES % GRID_W == 0 and NA_KROWS % (LANES // GRID_W) == 0
    sink_col = jnp.repeat(attn_sink.astype(jnp.float32) * LOG2E, SW_BLOCK, axis=1)[..., None]
    def per_layer_bf16(w):
        return [w[l].astype(bf16) for l in range(DEPTH)]

    p = {
        "pre_norm": pre_norm, "post_norm": post_norm, "mem_norm": mem_norm,
        "w_in": per_layer_bf16(w_in), "w_mem_kv": per_layer_bf16(w_mem_kv),
        "w_branch_a": per_layer_bf16(w_branch_a), "w_branch_b": per_layer_bf16(w_branch_b),
        "w_branch_m": per_layer_bf16(w_branch_m), "w_out": per_layer_bf16(w_out),
        "in_scale": jnp.asarray(_in_proj_scale()),
        "kv_scale": jnp.ones((1, 2 * MEM_WIDTH), jnp.float32),
        "na_bias": _na_bias_tables(na_rpb, seq),
        "sw_bias": _sw_bias_table(t5_bias),
        "sink_col": sink_col,
    }
    y_prompt = _trunk(x_prompt, mem_prompt, p)
    y_sample = _trunk(x_sample, mem_sample, p)
    return (y_prompt, y_sample)
```

```python
import functools

import numpy as np
import jax
import jax.numpy as jnp
from jax import lax
from jax.experimental import pallas as pl
from jax.experimental.pallas import tpu as pltpu

D_MODEL = 4096
DEPTH = 2
GRID_W = 64
HEAD_DIM = 128
NA_HEADS = 12
NA_ROWS = 8
NA_COLS = 16
SW_HEADS = 12
SW_KV_HEADS = 4
SW_GROUP = SW_HEADS // SW_KV_HEADS
SW_WINDOW = 128
SW_BLOCK = 128
T5_BUCKETS = 32
T5_MAX_DIST = 128
MEM_HEADS = 4
MEM_HEAD_DIM = 256
NA_WIDTH = NA_HEADS * HEAD_DIM
SW_WIDTH = SW_HEADS * HEAD_DIM
SW_KV_WIDTH = SW_KV_HEADS * HEAD_DIM
MEM_WIDTH = MEM_HEADS * MEM_HEAD_DIM
IN_SPLITS = (NA_WIDTH, NA_WIDTH, NA_WIDTH, NA_WIDTH,
             SW_WIDTH, SW_KV_WIDTH, SW_KV_WIDTH, SW_WIDTH,
             MEM_WIDTH, MEM_WIDTH,
             D_MODEL, D_MODEL, D_MODEL)
IN_WIDTH = sum(IN_SPLITS)
(OFF_QA, OFF_KA, OFF_VA, OFF_ZA, OFF_QB, OFF_KB, OFF_VB, OFF_ZB,
 OFF_QM, OFF_ZM, OFF_GA, OFF_GB, OFF_GM) = [int(c) for c in np.cumsum((0,) + IN_SPLITS[:-1])]
RMS_EPS = 1e-6
NEG_INF = -1e30
LOG2E = float(np.log2(np.e))

LANES = 128
VMEM_LIMIT_BYTES = 56 * 1024 * 1024
NORM_ROWS = 256
MM_BM = 1024
MM_BN = 1024
IN_BM = 512
IN_BN = 1024
MERGE_BM = 1024
MERGE_BN = 512
OUT_BM = 1024
OUT_BN = 512
NA_QROWS = 4
NA_KROWS = 12
MEM_QBLOCK = 512

_NT_DIMS = (((1,), (1,)), ((), ()))


def _params(*semantics):
    return pltpu.CompilerParams(dimension_semantics=semantics, vmem_limit_bytes=VMEM_LIMIT_BYTES)


def _weighted_sum_and_total(e, v):
    hd = v.shape[1]
    r = jnp.dot(e, jnp.concatenate([v, jnp.ones_like(v)], axis=1), preferred_element_type=jnp.float32)
    return r[:, :hd], r[:, hd:]


def _sigmoid(x):
    return 0.5 * jnp.tanh(0.5 * x) + 0.5


def _rmsnorm_kernel(x_ref, g_ref, o_ref):
    x = x_ref[...]
    ms = jnp.mean(x * x, axis=-1, keepdims=True)
    o_ref[...] = ((x * lax.rsqrt(ms + RMS_EPS)) * g_ref[...]).astype(o_ref.dtype)


def _rmsnorm(x2d, gain):
    rows, d = x2d.shape
    return pl.pallas_call(
        _rmsnorm_kernel,
        grid=(rows // NORM_ROWS,),
        in_specs=[pl.BlockSpec((NORM_ROWS, d), lambda i: (i, 0)),
                  pl.BlockSpec((1, d), lambda i: (0, 0))],
        out_specs=pl.BlockSpec((NORM_ROWS, d), lambda i: (i, 0)),
        out_shape=jax.ShapeDtypeStruct((rows, d), jnp.bfloat16),
        compiler_params=_params("parallel"),
        name="rmsnorm",
    )(x2d, gain.reshape(1, d))


def _matmul_kernel(x_ref, w_ref, s_ref, o_ref):
    acc = jnp.dot(x_ref[...], w_ref[...], preferred_element_type=jnp.float32)
    o_ref[...] = (acc * s_ref[...]).astype(o_ref.dtype)


def _matmul(x2d, w_all, layer, col_scale):
    m, k = x2d.shape
    n = w_all.shape[2]
    bm = min(MM_BM, m)
    bn = min(MM_BN, n)
    return pl.pallas_call(
        _matmul_kernel,
        grid=(m // bm, n // bn),
        in_specs=[pl.BlockSpec((bm, k), lambda i, j: (i, 0)),
                  pl.BlockSpec((None, k, bn), lambda i, j: (layer, 0, j)),
                  pl.BlockSpec((1, bn), lambda i, j: (0, j))],
        out_specs=pl.BlockSpec((bm, bn), lambda i, j: (i, j)),
        out_shape=jax.ShapeDtypeStruct((m, n), jnp.bfloat16),
        compiler_params=_params("parallel", "arbitrary"),
        name="proj_matmul",
    )(x2d, w_all, col_scale)


def _in_proj_kernel(x_ref, w_ref, s_ref, o_ref, wb_ref):
    @pl.when(pl.program_id(1) == 0)
    def _():
        wb_ref[...] = w_ref[...].astype(wb_ref.dtype)

    acc = jnp.dot(x_ref[...], wb_ref[...], preferred_element_type=jnp.float32)
    o_ref[...] = (acc * s_ref[...]).astype(o_ref.dtype)


def _in_proj(x2d, w_all, layer, col_scale):
    m, k = x2d.shape
    n = w_all.shape[2]
    bm = min(IN_BM, m)
    bn = min(IN_BN, n)
    return pl.pallas_call(
        _in_proj_kernel,
        grid=(n // bn, m // bm),
        in_specs=[pl.BlockSpec((bm, k), lambda j, i: (i, 0)),
                  pl.BlockSpec((None, k, bn), lambda j, i: (layer, 0, j)),
                  pl.BlockSpec((1, bn), lambda j, i: (0, j))],
        out_specs=pl.BlockSpec((bm, bn), lambda j, i: (i, j)),
        out_shape=jax.ShapeDtypeStruct((m, n), jnp.bfloat16),
        scratch_shapes=[pltpu.VMEM((k, bn), jnp.bfloat16)],
        compiler_params=_params("arbitrary", "arbitrary"),
        name="in_proj",
    )(x2d, w_all, col_scale)


def _na_kernel(q_ref, k_ref, v_ref, z_ref, bias_ref, o_ref):
    seq = q_ref.shape[0]
    qb = NA_QROWS * GRID_W
    kb = NA_KROWS * GRID_W
    nblk = seq // qb
    grid_rows = seq // GRID_W

    for i in range(nblk):
        case = 0 if i == 0 else (2 if i == nblk - 1 else 1)
        q0 = i * qb
        k0 = min(max(NA_QROWS * i - NA_ROWS // 2, 0), grid_rows - NA_KROWS) * GRID_W
        q = q_ref[q0:q0 + qb, :]
        k = k_ref[k0:k0 + kb, :]
        v = v_ref[k0:k0 + kb, :]
        s = lax.dot_general(q, k, _NT_DIMS, preferred_element_type=jnp.float32) + bias_ref[case]
        mx = jnp.max(s, axis=-1, keepdims=True)
        num, tot = _weighted_sum_and_total(jnp.exp2(s - mx).astype(v.dtype), v)
        o = num / tot
        z = z_ref[q0:q0 + qb, :].astype(jnp.float32)
        o_ref[q0:q0 + qb, :] = (o * (z * _sigmoid(z))).astype(o_ref.dtype)


def _na_attention(proj, bias_all, layer):
    bsz, seq, _ = proj.shape
    qb = NA_QROWS * GRID_W
    kb = NA_KROWS * GRID_W

    def col(off):
        return pl.BlockSpec((None, seq, HEAD_DIM), lambda h, b: (b, 0, off // HEAD_DIM + h))

    return pl.pallas_call(
        _na_kernel,
        grid=(NA_HEADS, bsz),
        in_specs=[col(OFF_QA), col(OFF_KA), col(OFF_VA), col(OFF_ZA),
                  pl.BlockSpec((None, None, 3, qb, kb), lambda h, b: (layer, h, 0, 0, 0))],
        out_specs=pl.BlockSpec((None, seq, HEAD_DIM), lambda h, b: (b, 0, h)),
        out_shape=jax.ShapeDtypeStruct((bsz, seq, NA_WIDTH), jnp.bfloat16),
        compiler_params=_params("parallel", "arbitrary"),
        name="na_attention",
    )(proj, proj, proj, proj, bias_all)


def _na_bias_tables(rpb, seq):
    grid_rows = seq // GRID_W
    nblk = grid_rows // NA_QROWS
    win_rows = min(NA_ROWS, grid_rows)
    qc = np.arange(GRID_W)
    cstart = np.clip(qc - NA_COLS // 2, 0, GRID_W - NA_COLS)
    kc = np.arange(GRID_W)
    col_ok = (kc[None, :] >= cstart[:, None]) & (kc[None, :] < cstart[:, None] + NA_COLS)
    dc = np.clip(kc[None, :] - qc[:, None], -(NA_COLS - 1), NA_COLS - 1) + NA_COLS - 1
    row_ok, dr = [], []
    for i in (0, 1, nblk - 1):
        qr = NA_QROWS * i + np.arange(NA_QROWS)
        kr = np.clip(NA_QROWS * i - NA_ROWS // 2, 0, grid_rows - NA_KROWS) + np.arange(NA_KROWS)
        rs = np.clip(qr - win_rows // 2, 0, grid_rows - win_rows)
        ok = (kr[None, :] >= rs[:, None]) & (kr[None, :] < rs[:, None] + win_rows)
        row_ok.append(ok)
        dr.append(np.clip(kr[None, :] - qr[:, None] + NA_ROWS - 1, 0, 2 * NA_ROWS - 2))
    row_ok = np.stack(row_ok)
    dr = np.stack(dr)
    nd, ne = 2 * NA_ROWS - 1, 2 * NA_COLS - 1
    row_hot = (dr[..., None] == np.arange(nd)).astype(np.float32)
    col_hot = (dc[..., None] == np.arange(ne)).astype(np.float32)
    pair = LANES // GRID_W
    pair_hot = np.zeros((pair, ne, GRID_W, pair, GRID_W), np.float32)
    for b in range(pair):
        pair_hot[b, :, :, b, :] = np.transpose(col_hot, (2, 0, 1))
    pair_hot = pair_hot.reshape(pair * ne, GRID_W, LANES)
    mask = row_ok[:, :, None, :, None] & col_ok[None, None, :, None, :]
    mask_add = np.where(mask, 0.0, NEG_INF).astype(np.float32).reshape(
        3, NA_QROWS * GRID_W, NA_KROWS * GRID_W)
    nl, nh = rpb.shape[0], rpb.shape[1]
    g = jnp.einsum("sabd,lhde->lhsabe", jnp.asarray(row_hot), rpb.astype(jnp.float32),
                   precision=lax.Precision.HIGHEST)
    g = g.reshape(nl, nh, 3, NA_QROWS, NA_KROWS // pair, pair * ne)
    t = jnp.einsum("lhsapk,kxz->lhsaxpz", g, jnp.asarray(pair_hot), precision=lax.Precision.HIGHEST)
    t = t.reshape(nl, nh, 3, NA_QROWS * GRID_W, NA_KROWS * GRID_W)
    return t * LOG2E + jnp.asarray(mask_add)[None, None]


def _sw_kernel(q_ref, k_ref, v_ref, z0_ref, z1_ref, z2_ref, bias_ref, sink_ref, o_ref):
    z_refs = (z0_ref, z1_ref, z2_ref)
    seq = k_ref.shape[0]
    blk = SW_BLOCK
    nblk = seq // blk
    sk = sink_ref[...]

    for i in range(nblk):
        r0 = i * blk
        lo = max(i - 1, 0) * blk
        hi = min(i + 2, nblk) * blk
        c0 = lo - (i - 1) * blk
        q = q_ref[r0:r0 + blk, :]
        qs = jnp.concatenate([q[:, g * HEAD_DIM:(g + 1) * HEAD_DIM] for g in range(SW_GROUP)], axis=0)
        kw = k_ref[lo:hi, :]
        vw = v_ref[lo:hi, :]
        s = (lax.dot_general(qs, kw, _NT_DIMS, preferred_element_type=jnp.float32)
             + bias_ref[:, c0:c0 + hi - lo])
        mx = jnp.maximum(jnp.max(s, axis=-1, keepdims=True), sk)
        num, tot = _weighted_sum_and_total(jnp.exp2(s - mx).astype(vw.dtype), vw)
        o = num / (tot + jnp.exp2(sk - mx))
        og =jnp.concatenate([o[g * blk:(g + 1) * blk, :] for g in range(SW_GROUP)], axis=1)
        z = jnp.concatenate([r[r0:r0 + blk, :] for r in z_refs], axis=1).astype(jnp.float32)
        o_ref[r0:r0 + blk, :] = (og * (z * _sigmoid(z))).astype(o_ref.dtype)


def _sw_attention(proj, bias, sink_col):
    bsz, seq, _ = proj.shape
    gw = SW_GROUP * HEAD_DIM

    def wide(off):
        return pl.BlockSpec((None, seq, gw), lambda k, b: (b, 0, off // gw + k))

    def narrow(off):
        return pl.BlockSpec((None, seq, HEAD_DIM), lambda k, b: (b, 0, off // HEAD_DIM + k))

    def gate(g):
        return pl.BlockSpec((None, seq, HEAD_DIM),
                            lambda k, b: (b, 0, OFF_ZB // HEAD_DIM + k * SW_GROUP + g))

    assert OFF_QB % gw == 0 and SW_GROUP == 3
    return pl.pallas_call(
        _sw_kernel,
        grid=(SW_KV_HEADS, bsz),
        in_specs=[wide(OFF_QB), narrow(OFF_KB), narrow(OFF_VB), gate(0), gate(1), gate(2),
                  pl.BlockSpec((SW_GROUP * SW_BLOCK, 3 * SW_BLOCK), lambda k, b: (k, 0)),
                  pl.BlockSpec((SW_GROUP * SW_BLOCK, 1), lambda k, b: (k, 0))],
        out_specs=pl.BlockSpec((None, seq, gw), lambda k, b: (b, 0, k)),
        out_shape=jax.ShapeDtypeStruct((bsz, seq, SW_WIDTH), jnp.bfloat16),
        compiler_params=_params("parallel", "arbitrary"),
        name="sw_attention",
    )(proj, proj, proj, proj, proj, proj, bias, sink_col)


def _t5_bucket_index(rel):
    nb = T5_BUCKETS // 2
    max_exact = nb // 2
    ret = (rel > 0).astype(np.int32) * nb
    n = np.abs(rel)
    large = max_exact + (np.log(np.maximum(n, 1) / max_exact)
                         / np.log(T5_MAX_DIST / max_exact) * (nb - max_exact)).astype(np.int32)
    large = np.minimum(large, nb - 1)
    return (ret + np.where(n < max_exact, n, large)).astype(np.int32)


def _sw_bias_table(t5_bias):
    rel = np.arange(3 * SW_BLOCK)[None, :] - SW_BLOCK - np.arange(SW_BLOCK)[:, None]
    band = np.abs(rel) <= SW_WINDOW
    hot = (_t5_bucket_index(rel)[..., None] == np.arange(T5_BUCKETS)).astype(np.float32)
    t = jnp.einsum("qkb,bh->hqk", jnp.asarray(hot), t5_bias.astype(jnp.float32),
                   precision=lax.Precision.HIGHEST)
    t = jnp.where(jnp.asarray(band)[None], t * LOG2E, NEG_INF)
    return t.reshape(SW_HEADS * SW_BLOCK, 3 * SW_BLOCK)


def _mem_kernel(q_ref, z_ref, k_ref, v_ref, o_ref):
    seq = q_ref.shape[0]
    k = k_ref[...]
    v = v_ref[...]

    for i in range(seq // MEM_QBLOCK):
        r0 = i * MEM_QBLOCK
        q = q_ref[r0:r0 + MEM_QBLOCK, :]
        s = lax.dot_general(q, k, _NT_DIMS, preferred_element_type=jnp.float32)
        mx = jnp.max(s, axis=-1, keepdims=True)
        e = jnp.exp2(s - mx)
        den = jnp.sum(e, axis=-1, keepdims=True)
        o = jnp.dot(e.astype(v.dtype), v, preferred_element_type=jnp.float32) / den
        z = z_ref[r0:r0 + MEM_QBLOCK, :].astype(jnp.float32)
        o_ref[r0:r0 + MEM_QBLOCK, :] = (o * (z * _sigmoid(z))).astype(o_ref.dtype)


def _mem_attention(proj, memkv):
    bsz, seq, _ = proj.shape
    mtok = memkv.shape[1]

    def qcol(off):
        return pl.BlockSpec((None, seq, MEM_HEAD_DIM), lambda h, b: (b, 0, off // MEM_HEAD_DIM + h))

    def kvcol(off):
        return pl.BlockSpec((None, mtok, MEM_HEAD_DIM), lambda h, b: (b, 0, off // MEM_HEAD_DIM + h))

    assert OFF_QM % MEM_HEAD_DIM == 0 and OFF_ZM % MEM_HEAD_DIM == 0
    return pl.pallas_call(
        _mem_kernel,
        grid=(MEM_HEADS, bsz),
        in_specs=[qcol(OFF_QM), qcol(OFF_ZM), kvcol(0), kvcol(MEM_WIDTH)],
        out_specs=pl.BlockSpec((None, seq, MEM_HEAD_DIM), lambda h, b: (b, 0, h)),
        out_shape=jax.ShapeDtypeStruct((bsz, seq, MEM_WIDTH), jnp.bfloat16),
        compiler_params=_params("parallel", "arbitrary"),
        name="mem_attention",
    )(proj, proj, memkv, memkv)


def _merge_kernel(a_ref, b_ref, m_ref, wa_ref, wb_ref, wm_ref, ga_ref, gb_ref, gm_ref, o_ref):
    def branch(x_ref, w_ref, g_ref):
        y = jnp.dot(x_ref[...], w_ref[...], preferred_element_type=jnp.float32)
        return _sigmoid(g_ref[...].astype(jnp.float32)) * y

    acc = branch(a_ref, wa_ref, ga_ref)
    acc = acc + branch(b_ref, wb_ref, gb_ref)
    acc = acc + branch(m_ref, wm_ref, gm_ref)
    o_ref[...] = acc.astype(o_ref.dtype)


def _merge(a2d, b2d, m2d, wa, wb, wm, layer, proj2d):
    rows = a2d.shape[0]
    bm = min(MERGE_BM, rows)
    bn = MERGE_BN

    def act(width):
        return pl.BlockSpec((bm, width), lambda i, j: (i, 0))

    def wgt(width):
        return pl.BlockSpec((None, width, bn), lambda i, j: (layer, 0, j))

    def gate(off):
        return pl.BlockSpec((bm, bn), lambda i, j: (i, off // bn + j))

    assert OFF_GA % bn == 0 and OFF_GB % bn == 0 and OFF_GM % bn == 0 and rows % bm == 0
    return pl.pallas_call(
        _merge_kernel,
        grid=(rows // bm, D_MODEL // bn),
        in_specs=[act(NA_WIDTH), act(SW_WIDTH), act(MEM_WIDTH),
                  wgt(NA_WIDTH), wgt(SW_WIDTH), wgt(MEM_WIDTH),
                  gate(OFF_GA), gate(OFF_GB), gate(OFF_GM)],
        out_specs=pl.BlockSpec((bm, bn), lambda i, j: (i, j)),
        out_shape=jax.ShapeDtypeStruct((rows, D_MODEL), jnp.bfloat16),
        compiler_params=_params("parallel", "arbitrary"),
        name="branch_merge",
    )(a2d, b2d, m2d, wa, wb, wm, proj2d, proj2d, proj2d)


def _outproj_kernel(mg_ref, w_ref, x_ref, g_ref, o_ref, y_ref, ssq_ref):
    i = pl.program_id(0)
    j = pl.program_id(1)
    n_tiles = pl.num_programs(0) - 1
    bn = w_ref.shape[1]
    d = y_ref.shape[1]
    col = pl.multiple_of(j * bn, bn)
    cur = lax.rem(i, 2)

    def finish():
        inv = lax.rsqrt(ssq_ref[1 - cur] * (1.0 / d) + RMS_EPS)
        o_ref[...] = x_ref[...] + (y_ref[:, pl.ds(col, bn)] * inv) * g_ref[...]

    def project():
        y = jnp.dot(mg_ref[...], w_ref[...], preferred_element_type=jnp.float32)
        y_ref[:, pl.ds(col, bn)] = y
        part = jnp.sum(y * y, axis=-1, keepdims=True)
        ssq_ref[cur] = part + jnp.where(j == 0, 0.0, ssq_ref[cur])

    @pl.when(i == 0)
    def _():
        @pl.when(j == 0)
        def _():
            ssq_ref[...] = jnp.zeros_like(ssq_ref)
        project()

    @pl.when((i > 0) & (i < n_tiles))
    def _():
        finish()
        project()

    @pl.when(i == n_tiles)
    def _():
        finish()


def _outproj(merged2d, w_all, layer, x2d, gain):
    rows, d = x2d.shape
    bm = min(OUT_BM, rows)
    bn = OUT_BN
    n_tiles = rows // bm

    def finished_block(i, j):
        return (jnp.maximum(i - 1, 0), jnp.where(i == 0, 0, j))

    return pl.pallas_call(
        _outproj_kernel,
        grid=(n_tiles + 1, d // bn),
        in_specs=[pl.BlockSpec((bm, d), lambda i, j: (jnp.minimum(i, n_tiles - 1), 0)),
                  pl.BlockSpec((None, d, bn), lambda i, j: (layer, 0, j)),
                  pl.BlockSpec((bm, bn), finished_block),
                  pl.BlockSpec((1, bn), lambda i, j: (0, j))],
        out_specs=pl.BlockSpec((bm, bn), finished_block),
        out_shape=jax.ShapeDtypeStruct((rows, d), jnp.float32),
        scratch_shapes=[pltpu.VMEM((bm, d), jnp.float32),
                        pltpu.VMEM((2, bm, 1), jnp.float32)],
        compiler_params=_params("arbitrary", "arbitrary"),
        name="out_proj_norm",
    )(merged2d, w_all, x2d, gain.reshape(1, d))


def _in_proj_scale():
    s = np.ones((1, IN_WIDTH), np.float32)
    s[0, OFF_QA:OFF_QA + NA_WIDTH] = HEAD_DIM ** -0.5 * LOG2E
    s[0, OFF_QB:OFF_QB + SW_WIDTH] = HEAD_DIM ** -0.5 * LOG2E
    s[0, OFF_QM:OFF_QM + MEM_WIDTH] = MEM_HEAD_DIM ** -0.5 * LOG2E
    return s


def _trunk(x, mem, p):
    bsz, seq, d = x.shape
    mtok = mem.shape[1]
    x2d = x.reshape(bsz * seq, d)
    mem2d = mem.reshape(bsz * mtok, d)
    for l in range(DEPTH):
        h = _rmsnorm(x2d, p["pre_norm"][l])
        proj2d = _in_proj(h, p["w_in"], l, p["in_scale"])
        proj = proj2d.reshape(bsz, seq, IN_WIDTH)
        mem_h = _rmsnorm(mem2d, p["mem_norm"][l])
        memkv = _matmul(mem_h, p["w_mem_kv"], l, p["kv_scale"]).reshape(bsz, mtok, 2 * MEM_WIDTH)
        a = _na_attention(proj, p["na_bias"], l)
        b = _sw_attention(proj, p["sw_bias"], p["sink_col"][l])
        m = _mem_attention(proj, memkv)
        merged = _merge(a.reshape(bsz * seq, NA_WIDTH), b.reshape(bsz * seq, SW_WIDTH),
                        m.reshape(bsz * seq, MEM_WIDTH), p["w_branch_a"], p["w_branch_b"],
                        p["w_branch_m"], l, proj2d)
        x2d = _outproj(merged, p["w_out"], l, x2d, p["post_norm"][l])
    return x2d.reshape(bsz, seq, d)


def kernel(x_prompt, x_sample, mem_prompt, mem_sample, pre_norm, post_norm, mem_norm, w_in, w_mem_kv,
           w_branch_a, w_branch_b, w_branch_m, w_out, na_rpb, attn_sink, t5_bias):
    bf16 = jnp.bfloat16
    seq = x_prompt.shape[1]
    assert x_sample.shape[1] == seq and seq % (NA_QROWS * GRID_W) == 0 and seq % SW_BLOCK == 0
    assert seq // GRID_W >= NA_KROWS and seq % MEM_QBLOCK == 0 and seq // SW_BLOCK >= 2
    assert LANES % GRID_W == 0 and NA_KROWS % (LANES // GRID_W) == 0
    sink_col = jnp.repeat(attn_sink.astype(jnp.float32) * LOG2E, SW_BLOCK, axis=1)[..., None]
    p = {
        "pre_norm": pre_norm, "post_norm": post_norm, "mem_norm": mem_norm,
        "w_in": w_in, "w_mem_kv": w_mem_kv.astype(bf16),
        "w_branch_a": w_branch_a.astype(bf16), "w_branch_b": w_branch_b.astype(bf16),
        "w_branch_m": w_branch_m.astype(bf16), "w_out": w_out.astype(bf16),
        "in_scale": jnp.asarray(_in_proj_scale()),
        "kv_scale": jnp.ones((1, 2 * MEM_WIDTH), jnp.float32),
        "na_bias": _na_bias_tables(na_rpb, seq),
        "sw_bias": _sw_bias_table(t5_bias),
        "sink_col": sink_col,
    }
    y_prompt = _trunk(x_prompt, mem_prompt, p)
    y_sample = _trunk(x_sample, mem_sample, p)
    return (y_prompt, y_sample)
```

```python
import functools

import numpy as np
import jax
import jax.numpy as jnp
from jax import lax
from jax.experimental import pallas as pl
from jax.experimental.pallas import tpu as pltpu

D_MODEL = 4096
DEPTH = 2
GRID_W = 64
HEAD_DIM = 128
NA_HEADS = 12
NA_ROWS = 8
NA_COLS = 16
SW_HEADS = 12
SW_KV_HEADS = 4
SW_GROUP = SW_HEADS // SW_KV_HEADS
SW_WINDOW = 128
SW_BLOCK = 128
T5_BUCKETS = 32
T5_MAX_DIST = 128
MEM_HEADS = 4
MEM_HEAD_DIM = 256
NA_WIDTH = NA_HEADS * HEAD_DIM
SW_WIDTH = SW_HEADS * HEAD_DIM
SW_KV_WIDTH = SW_KV_HEADS * HEAD_DIM
MEM_WIDTH = MEM_HEADS * MEM_HEAD_DIM
IN_SPLITS = (NA_WIDTH, NA_WIDTH, NA_WIDTH, NA_WIDTH,
             SW_WIDTH, SW_KV_WIDTH, SW_KV_WIDTH, SW_WIDTH,
             MEM_WIDTH, MEM_WIDTH,
             D_MODEL, D_MODEL, D_MODEL)
IN_WIDTH = sum(IN_SPLITS)
(OFF_QA, OFF_KA, OFF_VA, OFF_ZA, OFF_QB, OFF_KB, OFF_VB, OFF_ZB,
 OFF_QM, OFF_ZM, OFF_GA, OFF_GB, OFF_GM) = [int(c) for c in np.cumsum((0,) + IN_SPLITS[:-1])]
RMS_EPS = 1e-6
NEG_INF = -1e30
LOG2E = float(np.log2(np.e))

LANES = 128
VMEM_LIMIT_BYTES = 56 * 1024 * 1024
NORM_ROWS = 256
MM_BM = 1024
MM_BN = 1024
MERGE_BM = 1024
MERGE_BN = 512
OUT_BM = 1024
OUT_BN = 512
NA_QROWS = 4
NA_KROWS = 12
MEM_QBLOCK = 512

_NT_DIMS = (((1,), (1,)), ((), ()))


def _params(*semantics):
    return pltpu.CompilerParams(dimension_semantics=semantics, vmem_limit_bytes=VMEM_LIMIT_BYTES)


def _weighted_sum_and_total(e, v):
    hd = v.shape[1]
    r = jnp.dot(e, jnp.concatenate([v, jnp.ones_like(v)], axis=1), preferred_element_type=jnp.float32)
    return r[:, :hd], r[:, hd:]


def _sigmoid(x):
    return 0.5 * jnp.tanh(0.5 * x) + 0.5


def _rmsnorm_kernel(x_ref, g_ref, o_ref):
    x = x_ref[...]
    ms = jnp.mean(x * x, axis=-1, keepdims=True)
    o_ref[...] = ((x * lax.rsqrt(ms + RMS_EPS)) * g_ref[...]).astype(o_ref.dtype)


def _rmsnorm(x2d, gain):
    rows, d = x2d.shape
    return pl.pallas_call(
        _rmsnorm_kernel,
        grid=(rows // NORM_ROWS,),
        in_specs=[pl.BlockSpec((NORM_ROWS, d), lambda i: (i, 0)),
                  pl.BlockSpec((1, d), lambda i: (0, 0))],
        out_specs=pl.BlockSpec((NORM_ROWS, d), lambda i: (i, 0)),
        out_shape=jax.ShapeDtypeStruct((rows, d), jnp.bfloat16),
        compiler_params=_params("parallel"),
        name="rmsnorm",
    )(x2d, gain.reshape(1, d))


def _matmul_kernel(x_ref, w_ref, s_ref, o_ref):
    acc = jnp.dot(x_ref[...], w_ref[...], preferred_element_type=jnp.float32)
    o_ref[...] = (acc * s_ref[...]).astype(o_ref.dtype)


def _matmul(x2d, w_all, layer, col_scale):
    m, k = x2d.shape
    n = w_all.shape[2]
    bm = min(MM_BM, m)
    bn = min(MM_BN, n)
    return pl.pallas_call(
        _matmul_kernel,
        grid=(m // bm, n // bn),
        in_specs=[pl.BlockSpec((bm, k), lambda i, j: (i, 0)),
                  pl.BlockSpec((None, k, bn), lambda i, j: (layer, 0, j)),
                  pl.BlockSpec((1, bn), lambda i, j: (0, j))],
        out_specs=pl.BlockSpec((bm, bn), lambda i, j: (i, j)),
        out_shape=jax.ShapeDtypeStruct((m, n), jnp.bfloat16),
        compiler_params=_params("parallel", "arbitrary"),
        name="proj_matmul",
    )(x2d, w_all, col_scale)


def _na_kernel(q_ref, k_ref, v_ref, z_ref, bias_ref, o_ref):
    seq = q_ref.shape[0]
    qb = NA_QROWS * GRID_W
    kb = NA_KROWS * GRID_W
    nblk = seq // qb
    grid_rows = seq // GRID_W

    for i in range(nblk):
        case = 0 if i == 0 else (2 if i == nblk - 1 else 1)
        q0 = i * qb
        k0 = min(max(NA_QROWS * i - NA_ROWS // 2, 0), grid_rows - NA_KROWS) * GRID_W
        q = q_ref[q0:q0 + qb, :]
        k = k_ref[k0:k0 + kb, :]
        v = v_ref[k0:k0 + kb, :]
        s = lax.dot_general(q, k, _NT_DIMS, preferred_element_type=jnp.float32) + bias_ref[case]
        mx = jnp.max(s, axis=-1, keepdims=True)
        num, tot = _weighted_sum_and_total(jnp.exp2(s - mx).astype(v.dtype), v)
        o = num / tot
        z = z_ref[q0:q0 + qb, :].astype(jnp.float32)
        o_ref[q0:q0 + qb, :] = (o * (z * _sigmoid(z))).astype(o_ref.dtype)


def _na_attention(proj, bias_all, layer):
    bsz, seq, _ = proj.shape
    qb = NA_QROWS * GRID_W
    kb = NA_KROWS * GRID_W

    def col(off):
        return pl.BlockSpec((None, seq, HEAD_DIM), lambda h, b: (b, 0, off // HEAD_DIM + h))

    return pl.pallas_call(
        _na_kernel,
        grid=(NA_HEADS, bsz),
        in_specs=[col(OFF_QA), col(OFF_KA), col(OFF_VA), col(OFF_ZA),
                  pl.BlockSpec((None, None, 3, qb, kb), lambda h, b: (layer, h, 0, 0, 0))],
        out_specs=pl.BlockSpec((None, seq, HEAD_DIM), lambda h, b: (b, 0, h)),
        out_shape=jax.ShapeDtypeStruct((bsz, seq, NA_WIDTH), jnp.bfloat16),
        compiler_params=_params("parallel", "arbitrary"),
        name="na_attention",
    )(proj, proj, proj, proj, bias_all)


def _na_bias_tables(rpb, seq):
    grid_rows = seq // GRID_W
    nblk = grid_rows // NA_QROWS
    win_rows = min(NA_ROWS, grid_rows)
    qc = np.arange(GRID_W)
    cstart = np.clip(qc - NA_COLS // 2, 0, GRID_W - NA_COLS)
    kc = np.arange(GRID_W)
    col_ok = (kc[None, :] >= cstart[:, None]) & (kc[None, :] < cstart[:, None] + NA_COLS)
    dc = np.clip(kc[None, :] - qc[:, None], -(NA_COLS - 1), NA_COLS - 1) + NA_COLS - 1
    row_ok, dr = [], []
    for i in (0, 1, nblk - 1):
        qr = NA_QROWS * i + np.arange(NA_QROWS)
        kr = np.clip(NA_QROWS * i - NA_ROWS // 2, 0, grid_rows - NA_KROWS) + np.arange(NA_KROWS)
        rs = np.clip(qr - win_rows // 2, 0, grid_rows - win_rows)
        ok = (kr[None, :] >= rs[:, None]) & (kr[None, :] < rs[:, None] + win_rows)
        row_ok.append(ok)
        dr.append(np.clip(kr[None, :] - qr[:, None] + NA_ROWS - 1, 0, 2 * NA_ROWS - 2))
    row_ok = np.stack(row_ok)
    dr = np.stack(dr)
    nd, ne = 2 * NA_ROWS - 1, 2 * NA_COLS - 1
    row_hot = (dr[..., None] == np.arange(nd)).astype(np.float32)
    col_hot = (dc[..., None] == np.arange(ne)).astype(np.float32)
    pair = LANES // GRID_W
    pair_hot = np.zeros((pair, ne, GRID_W, pair, GRID_W), np.float32)
    for b in range(pair):
        pair_hot[b, :, :, b, :] = np.transpose(col_hot, (2, 0, 1))
    pair_hot = pair_hot.reshape(pair * ne, GRID_W, LANES)
    mask = row_ok[:, :, None, :, None] & col_ok[None, None, :, None, :]
    mask_add = np.where(mask, 0.0, NEG_INF).astype(np.float32).reshape(
        3, NA_QROWS * GRID_W, NA_KROWS * GRID_W)
    nl, nh = rpb.shape[0], rpb.shape[1]
    g = jnp.einsum("sabd,lhde->lhsabe", jnp.asarray(row_hot), rpb.astype(jnp.float32),
                   precision=lax.Precision.HIGHEST)
    g = g.reshape(nl, nh, 3, NA_QROWS, NA_KROWS // pair, pair * ne)
    t = jnp.einsum("lhsapk,kxz->lhsaxpz", g, jnp.asarray(pair_hot), precision=lax.Precision.HIGHEST)
    t = t.reshape(nl, nh, 3, NA_QROWS * GRID_W, NA_KROWS * GRID_W)
    return t * LOG2E + jnp.asarray(mask_add)[None, None]


def _sw_kernel(q_ref, k_ref, v_ref, z0_ref, z1_ref, z2_ref, bias_ref, sink_ref, o_ref):
    z_refs = (z0_ref, z1_ref, z2_ref)
    seq = k_ref.shape[0]
    blk = SW_BLOCK
    nblk = seq // blk
    sk = sink_ref[...]

    for i in range(nblk):
        r0 = i * blk
        lo = max(i - 1, 0) * blk
        hi = min(i + 2, nblk) * blk
        c0 = lo - (i - 1) * blk
        q = q_ref[r0:r0 + blk, :]
        qs = jnp.concatenate([q[:, g * HEAD_DIM:(g + 1) * HEAD_DIM] for g in range(SW_GROUP)], axis=0)
        kw = k_ref[lo:hi, :]
        vw = v_ref[lo:hi, :]
        s = (lax.dot_general(qs, kw, _NT_DIMS, preferred_element_type=jnp.float32)
             + bias_ref[:, c0:c0 + hi - lo])
        mx = jnp.maximum(jnp.max(s, axis=-1, keepdims=True), sk)
        mx_wide = jnp.concatenate([mx] * ((hi - lo) // HEAD_DIM), axis=1)
        num, tot = _weighted_sum_and_total(jnp.exp2(s - mx_wide).astype(vw.dtype), vw)
        o = num / (tot + jnp.exp2(sk - mx))
        og =jnp.concatenate([o[g * blk:(g + 1) * blk, :] for g in range(SW_GROUP)], axis=1)
        z = jnp.concatenate([r[r0:r0 + blk, :] for r in z_refs], axis=1).astype(jnp.float32)
        o_ref[r0:r0 + blk, :] = (og * (z * _sigmoid(z))).astype(o_ref.dtype)


def _sw_attention(proj, bias, sink_col):
    bsz, seq, _ = proj.shape
    gw = SW_GROUP * HEAD_DIM

    def wide(off):
        return pl.BlockSpec((None, seq, gw), lambda k, b: (b, 0, off // gw + k))

    def narrow(off):
        return pl.BlockSpec((None, seq, HEAD_DIM), lambda k, b: (b, 0, off // HEAD_DIM + k))

    def gate(g):
        return pl.BlockSpec((None, seq, HEAD_DIM),
                            lambda k, b: (b, 0, OFF_ZB // HEAD_DIM + k * SW_GROUP + g))

    assert OFF_QB % gw == 0 and SW_GROUP == 3
    return pl.pallas_call(
        _sw_kernel,
        grid=(SW_KV_HEADS, bsz),
        in_specs=[wide(OFF_QB), narrow(OFF_KB), narrow(OFF_VB), gate(0), gate(1), gate(2),
                  pl.BlockSpec((SW_GROUP * SW_BLOCK, 3 * SW_BLOCK), lambda k, b: (k, 0)),
                  pl.BlockSpec((SW_GROUP * SW_BLOCK, HEAD_DIM), lambda k, b: (k, 0))],
        out_specs=pl.BlockSpec((None, seq, gw), lambda k, b: (b, 0, k)),
        out_shape=jax.ShapeDtypeStruct((bsz, seq, SW_WIDTH), jnp.bfloat16),
        compiler_params=_params("parallel", "arbitrary"),
        name="sw_attention",
    )(proj, proj, proj, proj, proj, proj, bias, sink_col)


def _t5_bucket_index(rel):
    nb = T5_BUCKETS // 2
    max_exact = nb // 2
    ret = (rel > 0).astype(np.int32) * nb
    n = np.abs(rel)
    large = max_exact + (np.log(np.maximum(n, 1) / max_exact)
                         / np.log(T5_MAX_DIST / max_exact) * (nb - max_exact)).astype(np.int32)
    large = np.minimum(large, nb - 1)
    return (ret + np.where(n < max_exact, n, large)).astype(np.int32)


def _sw_bias_table(t5_bias):
    rel = np.arange(3 * SW_BLOCK)[None, :] - SW_BLOCK - np.arange(SW_BLOCK)[:, None]
    band = np.abs(rel) <= SW_WINDOW
    hot = (_t5_bucket_index(rel)[..., None] == np.arange(T5_BUCKETS)).astype(np.float32)
    t = jnp.einsum("qkb,bh->hqk", jnp.asarray(hot), t5_bias.astype(jnp.float32),
                   precision=lax.Precision.HIGHEST)
    t = jnp.where(jnp.asarray(band)[None], t * LOG2E, NEG_INF)
    return t.reshape(SW_HEADS * SW_BLOCK, 3 * SW_BLOCK)


def _mem_kernel(q_ref, z_ref, k_ref, v_ref, o_ref):
    seq = q_ref.shape[0]
    k = k_ref[...]
    v = v_ref[...]

    for i in range(seq // MEM_QBLOCK):
        r0 = i * MEM_QBLOCK
        q = q_ref[r0:r0 + MEM_QBLOCK, :]
        s = lax.dot_general(q, k, _NT_DIMS, preferred_element_type=jnp.float32)
        mx = jnp.max(s, axis=-1, keepdims=True)
        e = jnp.exp2(s - mx)
        den = jnp.sum(e, axis=-1, keepdims=True)
        o = jnp.dot(e.astype(v.dtype), v, preferred_element_type=jnp.float32) / den
        z = z_ref[r0:r0 + MEM_QBLOCK, :].astype(jnp.float32)
        o_ref[r0:r0 + MEM_QBLOCK, :] = (o * (z * _sigmoid(z))).astype(o_ref.dtype)


def _mem_attention(proj, memkv):
    bsz, seq, _ = proj.shape
    mtok = memkv.shape[1]

    def qcol(off):
        return pl.BlockSpec((None, seq, MEM_HEAD_DIM), lambda h, b: (b, 0, off // MEM_HEAD_DIM + h))

    def kvcol(off):
        return pl.BlockSpec((None, mtok, MEM_HEAD_DIM), lambda h, b: (b, 0, off // MEM_HEAD_DIM + h))

    assert OFF_QM % MEM_HEAD_DIM == 0 and OFF_ZM % MEM_HEAD_DIM == 0
    return pl.pallas_call(
        _mem_kernel,
        grid=(MEM_HEADS, bsz),
        in_specs=[qcol(OFF_QM), qcol(OFF_ZM), kvcol(0), kvcol(MEM_WIDTH)],
        out_specs=pl.BlockSpec((None, seq, MEM_HEAD_DIM), lambda h, b: (b, 0, h)),
        out_shape=jax.ShapeDtypeStruct((bsz, seq, MEM_WIDTH), jnp.bfloat16),
        compiler_params=_params("parallel", "arbitrary"),
        name="mem_attention",
    )(proj, proj, memkv, memkv)


def _merge_kernel(a_ref, b_ref, m_ref, wa_ref, wb_ref, wm_ref, ga_ref, gb_ref, gm_ref, o_ref):
    def branch(x_ref, w_ref, g_ref):
        y = jnp.dot(x_ref[...], w_ref[...], preferred_element_type=jnp.float32)
        return _sigmoid(g_ref[...].astype(jnp.float32)) * y

    acc = branch(a_ref, wa_ref, ga_ref)
    acc = acc + branch(b_ref, wb_ref, gb_ref)
    acc = acc + branch(m_ref, wm_ref, gm_ref)
    o_ref[...] = acc.astype(o_ref.dtype)


def _merge(a2d, b2d, m2d, wa, wb, wm, layer, proj2d):
    rows = a2d.shape[0]
    bm = min(MERGE_BM, rows)
    bn = MERGE_BN

    def act(width):
        return pl.BlockSpec((bm, width), lambda i, j: (i, 0))

    def wgt(width):
        return pl.BlockSpec((None, width, bn), lambda i, j: (layer, 0, j))

    def gate(off):
        return pl.BlockSpec((bm, bn), lambda i, j: (i, off // bn + j))

    assert OFF_GA % bn == 0 and OFF_GB % bn == 0 and OFF_GM % bn == 0 and rows % bm == 0
    return pl.pallas_call(
        _merge_kernel,
        grid=(rows // bm, D_MODEL // bn),
        in_specs=[act(NA_WIDTH), act(SW_WIDTH), act(MEM_WIDTH),
                  wgt(NA_WIDTH), wgt(SW_WIDTH), wgt(MEM_WIDTH),
                  gate(OFF_GA), gate(OFF_GB), gate(OFF_GM)],
        out_specs=pl.BlockSpec((bm, bn), lambda i, j: (i, j)),
        out_shape=jax.ShapeDtypeStruct((rows, D_MODEL), jnp.bfloat16),
        compiler_params=_params("parallel", "arbitrary"),
        name="branch_merge",
    )(a2d, b2d, m2d, wa, wb, wm, proj2d, proj2d, proj2d)


def _outproj_kernel(mg_ref, w_ref, x_ref, g_ref, o_ref, y_ref, ssq_ref):
    i = pl.program_id(0)
    j = pl.program_id(1)
    n_tiles = pl.num_programs(0) - 1
    bn = w_ref.shape[1]
    d = y_ref.shape[1]
    col = pl.multiple_of(j * bn, bn)
    cur = lax.rem(i, 2)

    def finish():
        inv = lax.rsqrt(ssq_ref[1 - cur] * (1.0 / d) + RMS_EPS)
        o_ref[...] = x_ref[...] + (y_ref[:, pl.ds(col, bn)] * inv) * g_ref[...]

    def project():
        y = jnp.dot(mg_ref[...], w_ref[...], preferred_element_type=jnp.float32)
        y_ref[:, pl.ds(col, bn)] = y
        part = jnp.sum(y * y, axis=-1, keepdims=True)
        ssq_ref[cur] = part + jnp.where(j == 0, 0.0, ssq_ref[cur])

    @pl.when(i == 0)
    def _():
        @pl.when(j == 0)
        def _():
            ssq_ref[...] = jnp.zeros_like(ssq_ref)
        project()

    @pl.when((i > 0) & (i < n_tiles))
    def _():
        finish()
        project()

    @pl.when(i == n_tiles)
    def _():
        finish()


def _outproj(merged2d, w_all, layer, x2d, gain):
    rows, d = x2d.shape
    bm = min(OUT_BM, rows)
    bn = OUT_BN
    n_tiles = rows // bm

    def finished_block(i, j):
        return (jnp.maximum(i - 1, 0), jnp.where(i == 0, 0, j))

    return pl.pallas_call(
        _outproj_kernel,
        grid=(n_tiles + 1, d // bn),
        in_specs=[pl.BlockSpec((bm, d), lambda i, j: (jnp.minimum(i, n_tiles - 1), 0)),
                  pl.BlockSpec((None, d, bn), lambda i, j: (layer, 0, j)),
                  pl.BlockSpec((bm, bn), finished_block),
                  pl.BlockSpec((1, bn), lambda i, j: (0, j))],
        out_specs=pl.BlockSpec((bm, bn), finished_block),
        out_shape=jax.ShapeDtypeStruct((rows, d), jnp.float32),
        scratch_shapes=[pltpu.VMEM((bm, d), jnp.float32),
                        pltpu.VMEM((2, bm, 1), jnp.float32)],
        compiler_params=_params("arbitrary", "arbitrary"),
        name="out_proj_norm",
    )(merged2d, w_all, x2d, gain.reshape(1, d))


def _in_proj_scale():
    s = np.ones((1, IN_WIDTH), np.float32)
    s[0, OFF_QA:OFF_QA + NA_WIDTH] = HEAD_DIM ** -0.5 * LOG2E
    s[0, OFF_QB:OFF_QB + SW_WIDTH] = HEAD_DIM ** -0.5 * LOG2E
    s[0, OFF_QM:OFF_QM + MEM_WIDTH] = MEM_HEAD_DIM ** -0.5 * LOG2E
    return s


def _trunk(x, mem, p):
    bsz, seq, d = x.shape
    mtok = mem.shape[1]
    x2d = x.reshape(bsz * seq, d)
    mem2d = mem.reshape(bsz * mtok, d)
    for l in range(DEPTH):
        h = _rmsnorm(x2d, p["pre_norm"][l])
        proj2d = _matmul(h, p["w_in"], l, p["in_scale"])
        proj = proj2d.reshape(bsz, seq, IN_WIDTH)
        mem_h = _rmsnorm(mem2d, p["mem_norm"][l])
        memkv = _matmul(mem_h, p["w_mem_kv"], l, p["kv_scale"]).reshape(bsz, mtok, 2 * MEM_WIDTH)
        a = _na_attention(proj, p["na_bias"], l)
        b = _sw_attention(proj, p["sw_bias"], p["sink_col"][l])
        m = _mem_attention(proj, memkv)
        merged = _merge(a.reshape(bsz * seq, NA_WIDTH), b.reshape(bsz * seq, SW_WIDTH),
                        m.reshape(bsz * seq, MEM_WIDTH), p["w_branch_a"], p["w_branch_b"],
                        p["w_branch_m"], l, proj2d)
        x2d = _outproj(merged, p["w_out"], l, x2d, p["post_norm"][l])
    return x2d.reshape(bsz, seq, d)


def kernel(x_prompt, x_sample, mem_prompt, mem_sample, pre_norm, post_norm, mem_norm, w_in, w_mem_kv,
           w_branch_a, w_branch_b, w_branch_m, w_out, na_rpb, attn_sink, t5_bias):
    bf16 = jnp.bfloat16
    seq = x_prompt.shape[1]
    assert x_sample.shape[1] == seq and seq % (NA_QROWS * GRID_W) == 0 and seq % SW_BLOCK == 0
    assert seq // GRID_W >= NA_KROWS and seq % MEM_QBLOCK == 0 and seq // SW_BLOCK >= 2
    assert LANES % GRID_W == 0 and NA_KROWS % (LANES // GRID_W) == 0
    sink_col = jnp.repeat(attn_sink.astype(jnp.float32) * LOG2E, SW_BLOCK, axis=1)
    sink_col = jnp.broadcast_to(sink_col[..., None], sink_col.shape + (HEAD_DIM,))
    p = {
        "pre_norm": pre_norm, "post_norm": post_norm, "mem_norm": mem_norm,
        "w_in": w_in.astype(bf16), "w_mem_kv": w_mem_kv.astype(bf16),
        "w_branch_a": w_branch_a.astype(bf16), "w_branch_b": w_branch_b.astype(bf16),
        "w_branch_m": w_branch_m.astype(bf16), "w_out": w_out.astype(bf16),
        "in_scale": jnp.asarray(_in_proj_scale()),
        "kv_scale": jnp.ones((1, 2 * MEM_WIDTH), jnp.float32),
        "na_bias": _na_bias_tables(na_rpb, seq),
        "sw_bias": _sw_bias_table(t5_bias),
        "sink_col": sink_col,
    }
    y_prompt = _trunk(x_prompt, mem_prompt, p)
    y_sample = _trunk(x_sample, mem_sample, p)
    return (y_prompt, y_sample)
```

```python
import functools

import numpy as np
import jax
import jax.numpy as jnp
from jax import lax
from jax.experimental import pallas as pl
from jax.experimental.pallas import tpu as pltpu

D_MODEL = 4096
DEPTH = 2
GRID_W = 64
HEAD_DIM = 128
NA_HEADS = 12
NA_ROWS = 8
NA_COLS = 16
SW_HEADS = 12
SW_KV_HEADS = 4
SW_GROUP = SW_HEADS // SW_KV_HEADS
SW_WINDOW = 128
SW_BLOCK = 128
T5_BUCKETS = 32
T5_MAX_DIST = 128
MEM_HEADS = 4
MEM_HEAD_DIM = 256
NA_WIDTH = NA_HEADS * HEAD_DIM
SW_WIDTH = SW_HEADS * HEAD_DIM
SW_KV_WIDTH = SW_KV_HEADS * HEAD_DIM
MEM_WIDTH = MEM_HEADS * MEM_HEAD_DIM
IN_SPLITS = (NA_WIDTH, NA_WIDTH, NA_WIDTH, NA_WIDTH,
             SW_WIDTH, SW_KV_WIDTH, SW_KV_WIDTH, SW_WIDTH,
             MEM_WIDTH, MEM_WIDTH,
             D_MODEL, D_MODEL, D_MODEL)
IN_WIDTH = sum(IN_SPLITS)
(OFF_QA, OFF_KA, OFF_VA, OFF_ZA, OFF_QB, OFF_KB, OFF_VB, OFF_ZB,
 OFF_QM, OFF_ZM, OFF_GA, OFF_GB, OFF_GM) = [int(c) for c in np.cumsum((0,) + IN_SPLITS[:-1])]
RMS_EPS = 1e-6
NEG_INF = -1e30
LOG2E = float(np.log2(np.e))

LANES = 128
VMEM_LIMIT_BYTES = 56 * 1024 * 1024
NORM_ROWS = 512
CAST_ROWS = 512
CAST_COLS = 2048
NA_HEADS_PER_STEP = 2
MM_BM = 1024
MM_BN = 1024
MERGE_BM = 1024
MERGE_BN = 512
OUT_BM = 1024
OUT_BN = 512
NA_QROWS = 4
NA_KROWS = 12
MEM_QBLOCK = 512

_NT_DIMS = (((1,), (1,)), ((), ()))


def _params(*semantics):
    return pltpu.CompilerParams(dimension_semantics=semantics, vmem_limit_bytes=VMEM_LIMIT_BYTES)


def _weighted_sum_and_total(e, v):
    hd = v.shape[1]
    r = jnp.dot(e, jnp.concatenate([v, jnp.ones_like(v)], axis=1), preferred_element_type=jnp.float32)
    return r[:, :hd], r[:, hd:]


def _sigmoid(x):
    return 0.5 * jnp.tanh(0.5 * x) + 0.5


def _rmsnorm_kernel(x_ref, g_ref, o_ref):
    x = x_ref[...]
    ms = jnp.mean(x * x, axis=-1, keepdims=True)
    o_ref[...] = ((x * lax.rsqrt(ms + RMS_EPS)) * g_ref[...]).astype(o_ref.dtype)


def _rmsnorm(x2d, gain):
    rows, d = x2d.shape
    return pl.pallas_call(
        _rmsnorm_kernel,
        grid=(rows // NORM_ROWS,),
        in_specs=[pl.BlockSpec((NORM_ROWS, d), lambda i: (i, 0)),
                  pl.BlockSpec((1, d), lambda i: (0, 0))],
        out_specs=pl.BlockSpec((NORM_ROWS, d), lambda i: (i, 0)),
        out_shape=jax.ShapeDtypeStruct((rows, d), jnp.bfloat16),
        compiler_params=_params("parallel"),
        name="rmsnorm",
    )(x2d, gain.reshape(1, d))


def _matmul_kernel(x_ref, w_ref, s_ref, o_ref):
    acc = jnp.dot(x_ref[...], w_ref[...], preferred_element_type=jnp.float32)
    o_ref[...] = (acc * s_ref[...]).astype(o_ref.dtype)


def _matmul_cast_kernel(x_ref, w_ref, s_ref, src_ref, o_ref, dst_ref):
    _matmul_kernel(x_ref, w_ref, s_ref, o_ref)
    dst_ref[...] = src_ref[...].astype(dst_ref.dtype)


def _matmul(x2d, w_all, layer, col_scale, cast_src=None, cast_layer=None):
    m, k = x2d.shape
    n = w_all.shape[2]
    bm = min(MM_BM, m)
    bn = min(MM_BN, n)
    grid = (m // bm, n // bn)
    in_specs = [pl.BlockSpec((bm, k), lambda i, j: (i, 0)),
                pl.BlockSpec((None, k, bn), lambda i, j: (layer, 0, j)),
                pl.BlockSpec((1, bn), lambda i, j: (0, j))]
    out_spec = pl.BlockSpec((bm, bn), lambda i, j: (i, j))
    out_shape = jax.ShapeDtypeStruct((m, n), jnp.bfloat16)
    if cast_src is None:
        return pl.pallas_call(
            _matmul_kernel, grid=grid, in_specs=in_specs, out_specs=out_spec, out_shape=out_shape,
            compiler_params=_params("parallel", "arbitrary"), name="proj_matmul",
        )(x2d, w_all, col_scale)
    _, k2, n2 = cast_src.shape
    assert k2 % grid[0] == 0 and n2 % grid[1] == 0
    cb = (k2 // grid[0], n2 // grid[1])
    return pl.pallas_call(
        _matmul_cast_kernel, grid=grid,
        in_specs=in_specs + [pl.BlockSpec((None,) + cb, lambda i, j: (cast_layer, i, j))],
        out_specs=(out_spec, pl.BlockSpec((None,) + cb, lambda i, j: (0, i, j))),
        out_shape=(out_shape, jax.ShapeDtypeStruct((1, k2, n2), jnp.bfloat16)),
        compiler_params=_params("parallel", "arbitrary"), name="proj_matmul_cast",
    )(x2d, w_all, col_scale, cast_src)


def _cast_kernel(src_ref, dst_ref):
    dst_ref[...] = src_ref[...].astype(dst_ref.dtype)


def _cast_layer(w_all, layer):
    _, k, n = w_all.shape
    bk, bn = min(CAST_ROWS, k), min(CAST_COLS, n)
    return pl.pallas_call(
        _cast_kernel,
        grid=(k // bk, n // bn),
        in_specs=[pl.BlockSpec((None, bk, bn), lambda i, j: (layer, i, j))],
        out_specs=pl.BlockSpec((None, bk, bn), lambda i, j: (0, i, j)),
        out_shape=jax.ShapeDtypeStruct((1, k, n), jnp.bfloat16),
        compiler_params=_params("parallel", "parallel"),
        name="cast_weights",
    )(w_all)


def _na_kernel(q_ref, k_ref, v_ref, z_ref, bias_ref, o_ref):
    seq = q_ref.shape[0]
    qb = NA_QROWS * GRID_W
    kb = NA_KROWS * GRID_W
    nblk = seq // qb
    grid_rows = seq // GRID_W

    for hh in range(NA_HEADS_PER_STEP):
        lanes = slice(hh * HEAD_DIM, (hh + 1) * HEAD_DIM)
        for i in range(nblk):
            case = 0 if i == 0 else (2 if i == nblk - 1 else 1)
            q0 = i * qb
            k0 = min(max(NA_QROWS * i - NA_ROWS // 2, 0), grid_rows - NA_KROWS) * GRID_W
            q = q_ref[q0:q0 + qb, lanes]
            k = k_ref[k0:k0 + kb, lanes]
            v = v_ref[k0:k0 + kb, lanes]
            s = lax.dot_general(q, k, _NT_DIMS, preferred_element_type=jnp.float32) + bias_ref[hh, case]
            mx = jnp.max(s, axis=-1, keepdims=True)
            num, tot = _weighted_sum_and_total(jnp.exp2(s - mx).astype(v.dtype), v)
            o = num / tot
            z = z_ref[q0:q0 + qb, lanes].astype(jnp.float32)
            o_ref[q0:q0 + qb, lanes] = (o * (z * _sigmoid(z))).astype(o_ref.dtype)


def _na_attention(proj, bias_all, layer):
    bsz, seq, _ = proj.shape
    qb = NA_QROWS * GRID_W
    kb = NA_KROWS * GRID_W

    hps = NA_HEADS_PER_STEP
    width = hps * HEAD_DIM
    assert NA_HEADS % hps == 0 and all(off % width == 0 for off in (OFF_QA, OFF_KA, OFF_VA, OFF_ZA))

    def col(off):
        return pl.BlockSpec((None, seq, width), lambda h, b: (b, 0, off // width + h))

    return pl.pallas_call(
        _na_kernel,
        grid=(NA_HEADS // hps, bsz),
        in_specs=[col(OFF_QA), col(OFF_KA), col(OFF_VA), col(OFF_ZA),
                  pl.BlockSpec((None, hps, 3, qb, kb), lambda h, b: (layer, h, 0, 0, 0))],
        out_specs=pl.BlockSpec((None, seq, width), lambda h, b: (b, 0, h)),
        out_shape=jax.ShapeDtypeStruct((bsz, seq, NA_WIDTH), jnp.bfloat16),
        compiler_params=_params("parallel", "arbitrary"),
        name="na_attention",
    )(proj, proj, proj, proj, bias_all)


def _na_bias_tables(rpb, seq):
    grid_rows = seq // GRID_W
    nblk = grid_rows // NA_QROWS
    win_rows = min(NA_ROWS, grid_rows)
    qc = np.arange(GRID_W)
    cstart = np.clip(qc - NA_COLS // 2, 0, GRID_W - NA_COLS)
    kc = np.arange(GRID_W)
    col_ok = (kc[None, :] >= cstart[:, None]) & (kc[None, :] < cstart[:, None] + NA_COLS)
    dc = np.clip(kc[None, :] - qc[:, None], -(NA_COLS - 1), NA_COLS - 1) + NA_COLS - 1
    row_ok, dr = [], []
    for i in (0, 1, nblk - 1):
        qr = NA_QROWS * i + np.arange(NA_QROWS)
        kr = np.clip(NA_QROWS * i - NA_ROWS // 2, 0, grid_rows - NA_KROWS) + np.arange(NA_KROWS)
        rs = np.clip(qr - win_rows // 2, 0, grid_rows - win_rows)
        ok = (kr[None, :] >= rs[:, None]) & (kr[None, :] < rs[:, None] + win_rows)
        row_ok.append(ok)
        dr.append(np.clip(kr[None, :] - qr[:, None] + NA_ROWS - 1, 0, 2 * NA_ROWS - 2))
    row_ok = np.stack(row_ok)
    dr = np.stack(dr)
    nd, ne = 2 * NA_ROWS - 1, 2 * NA_COLS - 1
    row_hot = (dr[..., None] == np.arange(nd)).astype(np.float32)
    col_hot = (dc[..., None] == np.arange(ne)).astype(np.float32)
    pair = LANES // GRID_W
    pair_hot = np.zeros((pair, ne, GRID_W, pair, GRID_W), np.float32)
    for b in range(pair):
        pair_hot[b, :, :, b, :] = np.transpose(col_hot, (2, 0, 1))
    pair_hot = pair_hot.reshape(pair * ne, GRID_W, LANES)
    mask = row_ok[:, :, None, :, None] & col_ok[None, None, :, None, :]
    mask_add = np.where(mask, 0.0, NEG_INF).astype(np.float32).reshape(
        3, NA_QROWS * GRID_W, NA_KROWS * GRID_W)
    nl, nh = rpb.shape[0], rpb.shape[1]
    g = jnp.einsum("sabd,lhde->lhsabe", jnp.asarray(row_hot), rpb.astype(jnp.float32),
                   precision=lax.Precision.HIGHEST)
    g = g.reshape(nl, nh, 3, NA_QROWS, NA_KROWS // pair, pair * ne)
    t = jnp.einsum("lhsapk,kxz->lhsaxpz", g, jnp.asarray(pair_hot), precision=lax.Precision.HIGHEST)
    t = t.reshape(nl, nh, 3, NA_QROWS * GRID_W, NA_KROWS * GRID_W)
    return t * LOG2E + jnp.asarray(mask_add)[None, None]


def _sw_kernel(q_ref, k_ref, v_ref, z0_ref, z1_ref, z2_ref, bias_ref, sink_ref, o_ref):
    z_refs = (z0_ref, z1_ref, z2_ref)
    seq = k_ref.shape[0]
    blk = SW_BLOCK
    nblk = seq // blk
    sk = sink_ref[...]

    for i in range(nblk):
        r0 = i * blk
        lo = max(i - 1, 0) * blk
        hi = min(i + 2, nblk) * blk
        c0 = lo - (i - 1) * blk
        q = q_ref[r0:r0 + blk, :]
        qs = jnp.concatenate([q[:, g * HEAD_DIM:(g + 1) * HEAD_DIM] for g in range(SW_GROUP)], axis=0)
        kw = k_ref[lo:hi, :]
        vw = v_ref[lo:hi, :]
        s = (lax.dot_general(qs, kw, _NT_DIMS, preferred_element_type=jnp.float32)
             + bias_ref[:, c0:c0 + hi - lo])
        mx = jnp.maximum(jnp.max(s, axis=-1, keepdims=True), sk)
        mx_wide = jnp.concatenate([mx] * ((hi - lo) // HEAD_DIM), axis=1)
        num, tot = _weighted_sum_and_total(jnp.exp2(s - mx_wide).astype(vw.dtype), vw)
        o = num / (tot + jnp.exp2(sk - mx))
        og =jnp.concatenate([o[g * blk:(g + 1) * blk, :] for g in range(SW_GROUP)], axis=1)
        z = jnp.concatenate([r[r0:r0 + blk, :] for r in z_refs], axis=1).astype(jnp.float32)
        o_ref[r0:r0 + blk, :] = (og * (z * _sigmoid(z))).astype(o_ref.dtype)


def _sw_attention(proj, bias, sink_col):
    bsz, seq, _ = proj.shape
    gw = SW_GROUP * HEAD_DIM

    def wide(off):
        return pl.BlockSpec((None, seq, gw), lambda k, b: (b, 0, off // gw + k))

    def narrow(off):
        return pl.BlockSpec((None, seq, HEAD_DIM), lambda k, b: (b, 0, off // HEAD_DIM + k))

    def gate(g):
        return pl.BlockSpec((None, seq, HEAD_DIM),
                            lambda k, b: (b, 0, OFF_ZB // HEAD_DIM + k * SW_GROUP + g))

    assert OFF_QB % gw == 0 and SW_GROUP == 3
    return pl.pallas_call(
        _sw_kernel,
        grid=(SW_KV_HEADS, bsz),
        in_specs=[wide(OFF_QB), narrow(OFF_KB), narrow(OFF_VB), gate(0), gate(1), gate(2),
                  pl.BlockSpec((SW_GROUP * SW_BLOCK, 3 * SW_BLOCK), lambda k, b: (k, 0)),
                  pl.BlockSpec((SW_GROUP * SW_BLOCK, HEAD_DIM), lambda k, b: (k, 0))],
        out_specs=pl.BlockSpec((None, seq, gw), lambda k, b: (b, 0, k)),
        out_shape=jax.ShapeDtypeStruct((bsz, seq, SW_WIDTH), jnp.bfloat16),
        compiler_params=_params("parallel", "arbitrary"),
        name="sw_attention",
    )(proj, proj, proj, proj, proj, proj, bias, sink_col)


def _t5_bucket_index(rel):
    nb = T5_BUCKETS // 2
    max_exact = nb // 2
    ret = (rel > 0).astype(np.int32) * nb
    n = np.abs(rel)
    large = max_exact + (np.log(np.maximum(n, 1) / max_exact)
                         / np.log(T5_MAX_DIST / max_exact) * (nb - max_exact)).astype(np.int32)
    large = np.minimum(large, nb - 1)
    return (ret + np.where(n < max_exact, n, large)).astype(np.int32)


def _sw_bias_table(t5_bias):
    rel = np.arange(3 * SW_BLOCK)[None, :] - SW_BLOCK - np.arange(SW_BLOCK)[:, None]
    band = np.abs(rel) <= SW_WINDOW
    hot = (_t5_bucket_index(rel)[..., None] == np.arange(T5_BUCKETS)).astype(np.float32)
    t = jnp.einsum("qkb,bh->hqk", jnp.asarray(hot), t5_bias.astype(jnp.float32),
                   precision=lax.Precision.HIGHEST)
    t = jnp.where(jnp.asarray(band)[None], t * LOG2E, NEG_INF)
    return t.reshape(SW_HEADS * SW_BLOCK, 3 * SW_BLOCK)


def _mem_kernel(q_ref, z_ref, k_ref, v_ref, o_ref):
    seq = q_ref.shape[0]
    k = k_ref[...]
    v = v_ref[...]

    for i in range(seq // MEM_QBLOCK):
        r0 = i * MEM_QBLOCK
        q = q_ref[r0:r0 + MEM_QBLOCK, :]
        s = lax.dot_general(q, k, _NT_DIMS, preferred_element_type=jnp.float32)
        mx = jnp.max(s, axis=-1, keepdims=True)
        e = jnp.exp2(s - mx)
        den = jnp.sum(e, axis=-1, keepdims=True)
        o = jnp.dot(e.astype(v.dtype), v, preferred_element_type=jnp.float32) / den
        z = z_ref[r0:r0 + MEM_QBLOCK, :].astype(jnp.float32)
        o_ref[r0:r0 + MEM_QBLOCK, :] = (o * (z * _sigmoid(z))).astype(o_ref.dtype)


def _mem_attention(proj, memkv):
    bsz, seq, _ = proj.shape
    mtok = memkv.shape[1]

    def qcol(off):
        return pl.BlockSpec((None, seq, MEM_HEAD_DIM), lambda h, b: (b, 0, off // MEM_HEAD_DIM + h))

    def kvcol(off):
        return pl.BlockSpec((None, mtok, MEM_HEAD_DIM), lambda h, b: (b, 0, off // MEM_HEAD_DIM + h))

    assert OFF_QM % MEM_HEAD_DIM == 0 and OFF_ZM % MEM_HEAD_DIM == 0
    return pl.pallas_call(
        _mem_kernel,
        grid=(MEM_HEADS, bsz),
        in_specs=[qcol(OFF_QM), qcol(OFF_ZM), kvcol(0), kvcol(MEM_WIDTH)],
        out_specs=pl.BlockSpec((None, seq, MEM_HEAD_DIM), lambda h, b: (b, 0, h)),
        out_shape=jax.ShapeDtypeStruct((bsz, seq, MEM_WIDTH), jnp.bfloat16),
        compiler_params=_params("parallel", "arbitrary"),
        name="mem_attention",
    )(proj, proj, memkv, memkv)


def _merge_kernel(a_ref, b_ref, m_ref, wa_ref, wb_ref, wm_ref, ga_ref, gb_ref, gm_ref, o_ref):
    def branch(x_ref, w_ref, g_ref):
        y = jnp.dot(x_ref[...], w_ref[...], preferred_element_type=jnp.float32)
        return _sigmoid(g_ref[...].astype(jnp.float32)) * y

    acc = branch(a_ref, wa_ref, ga_ref)
    acc = acc + branch(b_ref, wb_ref, gb_ref)
    acc = acc + branch(m_ref, wm_ref, gm_ref)
    o_ref[...] = acc.astype(o_ref.dtype)


def _merge(a2d, b2d, m2d, wa, wb, wm, layer, proj2d):
    rows = a2d.shape[0]
    bm = min(MERGE_BM, rows)
    bn = MERGE_BN

    def act(width):
        return pl.BlockSpec((bm, width), lambda i, j: (i, 0))

    def wgt(width):
        return pl.BlockSpec((None, width, bn), lambda i, j: (layer, 0, j))

    def gate(off):
        return pl.BlockSpec((bm, bn), lambda i, j: (i, off // bn + j))

    assert OFF_GA % bn == 0 and OFF_GB % bn == 0 and OFF_GM % bn == 0 and rows % bm == 0
    return pl.pallas_call(
        _merge_kernel,
        grid=(rows // bm, D_MODEL // bn),
        in_specs=[act(NA_WIDTH), act(SW_WIDTH), act(MEM_WIDTH),
                  wgt(NA_WIDTH), wgt(SW_WIDTH), wgt(MEM_WIDTH),
                  gate(OFF_GA), gate(OFF_GB), gate(OFF_GM)],
        out_specs=pl.BlockSpec((bm, bn), lambda i, j: (i, j)),
        out_shape=jax.ShapeDtypeStruct((rows, D_MODEL), jnp.bfloat16),
        compiler_params=_params("parallel", "arbitrary"),
        name="branch_merge",
    )(a2d, b2d, m2d, wa, wb, wm, proj2d, proj2d, proj2d)


def _outproj_kernel(mg_ref, w_ref, x_ref, g_ref, o_ref, y_ref, ssq_ref):
    i = pl.program_id(0)
    j = pl.program_id(1)
    n_tiles = pl.num_programs(0) - 1
    bn = w_ref.shape[1]
    d = y_ref.shape[1]
    col = pl.multiple_of(j * bn, bn)
    cur = lax.rem(i, 2)

    def finish():
        inv = lax.rsqrt(ssq_ref[1 - cur] * (1.0 / d) + RMS_EPS)
        o_ref[...] = x_ref[...] + (y_ref[:, pl.ds(col, bn)] * inv) * g_ref[...]

    def project():
        y = jnp.dot(mg_ref[...], w_ref[...], preferred_element_type=jnp.float32)
        y_ref[:, pl.ds(col, bn)] = y
        part = jnp.sum(y * y, axis=-1, keepdims=True)
        ssq_ref[cur] = part + jnp.where(j == 0, 0.0, ssq_ref[cur])

    @pl.when(i == 0)
    def _():
        @pl.when(j == 0)
        def _():
            ssq_ref[...] = jnp.zeros_like(ssq_ref)
        project()

    @pl.when((i > 0) & (i < n_tiles))
    def _():
        finish()
        project()

    @pl.when(i == n_tiles)
    def _():
        finish()


def _outproj(merged2d, w_all, layer, x2d, gain):
    rows, d = x2d.shape
    bm = min(OUT_BM, rows)
    bn = OUT_BN
    n_tiles = rows // bm

    def finished_block(i, j):
        return (jnp.maximum(i - 1, 0), jnp.where(i == 0, 0, j))

    return pl.pallas_call(
        _outproj_kernel,
        grid=(n_tiles + 1, d // bn),
        in_specs=[pl.BlockSpec((bm, d), lambda i, j: (jnp.minimum(i, n_tiles - 1), 0)),
                  pl.BlockSpec((None, d, bn), lambda i, j: (layer, 0, j)),
                  pl.BlockSpec((bm, bn), finished_block),
                  pl.BlockSpec((1, bn), lambda i, j: (0, j))],
        out_specs=pl.BlockSpec((bm, bn), finished_block),
        out_shape=jax.ShapeDtypeStruct((rows, d), jnp.float32),
        scratch_shapes=[pltpu.VMEM((bm, d), jnp.float32),
                        pltpu.VMEM((2, bm, 1), jnp.float32)],
        compiler_params=_params("arbitrary", "arbitrary"),
        name="out_proj_norm",
    )(merged2d, w_all, x2d, gain.reshape(1, d))


def _in_proj_scale():
    s = np.ones((1, IN_WIDTH), np.float32)
    s[0, OFF_QA:OFF_QA + NA_WIDTH] = HEAD_DIM ** -0.5 * LOG2E
    s[0, OFF_QB:OFF_QB + SW_WIDTH] = HEAD_DIM ** -0.5 * LOG2E
    s[0, OFF_QM:OFF_QM + MEM_WIDTH] = MEM_HEAD_DIM ** -0.5 * LOG2E
    return s


def _trunk(x, mem, p):
    bsz, seq, d = x.shape
    mtok = mem.shape[1]
    x2d = x.reshape(bsz * seq, d)
    mem2d = mem.reshape(bsz * mtok, d)
    for l in range(DEPTH):
        h = _rmsnorm(x2d, p["pre_norm"][l])
        w_in_bf = p["w_in_bf16"]
        if l + 1 < DEPTH and w_in_bf[l + 1] is None:
            proj2d, w_in_bf[l + 1] = _matmul(h, w_in_bf[l], 0, p["in_scale"],
                                             cast_src=p["w_in"], cast_layer=l + 1)
        else:
            proj2d = _matmul(h, w_in_bf[l], 0, p["in_scale"])
        proj = proj2d.reshape(bsz, seq, IN_WIDTH)
        mem_h = _rmsnorm(mem2d, p["mem_norm"][l])
        memkv = _matmul(mem_h, p["w_mem_kv"], l, p["kv_scale"]).reshape(bsz, mtok, 2 * MEM_WIDTH)
        a = _na_attention(proj, p["na_bias"], l)
        b = _sw_attention(proj, p["sw_bias"], p["sink_col"][l])
        m = _mem_attention(proj, memkv)
        merged = _merge(a.reshape(bsz * seq, NA_WIDTH), b.reshape(bsz * seq, SW_WIDTH),
                        m.reshape(bsz * seq, MEM_WIDTH), p["w_branch_a"], p["w_branch_b"],
                        p["w_branch_m"], l, proj2d)
        x2d = _outproj(merged, p["w_out"], l, x2d, p["post_norm"][l])
    return x2d.reshape(bsz, seq, d)


def kernel(x_prompt, x_sample, mem_prompt, mem_sample, pre_norm, post_norm, mem_norm, w_in, w_mem_kv,
           w_branch_a, w_branch_b, w_branch_m, w_out, na_rpb, attn_sink, t5_bias):
    bf16 = jnp.bfloat16
    seq = x_prompt.shape[1]
    assert x_sample.shape[1] == seq and seq % (NA_QROWS * GRID_W) == 0 and seq % SW_BLOCK == 0
    assert seq // GRID_W >= NA_KROWS and seq % MEM_QBLOCK == 0 and seq // SW_BLOCK >= 2
    assert LANES % GRID_W == 0 and NA_KROWS % (LANES // GRID_W) == 0
    sink_col = jnp.repeat(attn_sink.astype(jnp.float32) * LOG2E, SW_BLOCK, axis=1)
    sink_col = jnp.broadcast_to(sink_col[..., None], sink_col.shape + (HEAD_DIM,))
    p = {
        "pre_norm": pre_norm, "post_norm": post_norm, "mem_norm": mem_norm,
        "w_in": w_in, "w_in_bf16": [_cast_layer(w_in, 0)] + [None] * (DEPTH - 1),
        "w_mem_kv": w_mem_kv.astype(bf16),
        "w_branch_a": w_branch_a.astype(bf16), "w_branch_b": w_branch_b.astype(bf16),
        "w_branch_m": w_branch_m.astype(bf16), "w_out": w_out.astype(bf16),
        "in_scale": jnp.asarray(_in_proj_scale()),
        "kv_scale": jnp.ones((1, 2 * MEM_WIDTH), jnp.float32),
        "na_bias": _na_bias_tables(na_rpb, seq),
        "sw_bias": _sw_bias_table(t5_bias),
        "sink_col": sink_col,
    }
    y_prompt = _trunk(x_prompt, mem_prompt, p)
    y_sample = _trunk(x_sample, mem_sample, p)
    return (y_prompt, y_sample)
```

```python
import functools
import math

import numpy as np
import jax
import jax.numpy as jnp
from jax import lax
from jax.experimental import pallas as pl
from jax.experimental.pallas import tpu as pltpu

D_MODEL = 4096
DEPTH = 2
GRID_W = 64
HEAD_DIM = 128
NA_HEADS = 12
NA_ROWS = 8
NA_COLS = 16
SW_HEADS = 12
SW_KV_HEADS = 4
SW_GROUP = SW_HEADS // SW_KV_HEADS
SW_WINDOW = 128
SW_BLOCK = 128
T5_BUCKETS = 32
T5_MAX_DIST = 128
MEM_HEADS = 4
MEM_HEAD_DIM = 256
NA_WIDTH = NA_HEADS * HEAD_DIM
SW_WIDTH = SW_HEADS * HEAD_DIM
SW_KV_WIDTH = SW_KV_HEADS * HEAD_DIM
MEM_WIDTH = MEM_HEADS * MEM_HEAD_DIM
IN_SPLITS = (NA_WIDTH, NA_WIDTH, NA_WIDTH, NA_WIDTH,
             SW_WIDTH, SW_KV_WIDTH, SW_KV_WIDTH, SW_WIDTH,
             MEM_WIDTH, MEM_WIDTH,
             D_MODEL, D_MODEL, D_MODEL)
IN_WIDTH = sum(IN_SPLITS)
(OFF_QA, OFF_KA, OFF_VA, OFF_ZA, OFF_QB, OFF_KB, OFF_VB, OFF_ZB,
 OFF_QM, OFF_ZM, OFF_GA, OFF_GB, OFF_GM) = [int(c) for c in np.cumsum((0,) + IN_SPLITS[:-1])]
RMS_EPS = 1e-6
NEG_INF = -1e30
LOG2E = float(np.log2(np.e))

LANES = 128
BF16_SUBLANES = 16
VMEM_LIMIT_BYTES = 60 * 1024 * 1024
NORM_ROWS = 512
CAST_ROWS = 512
CAST_COLS = 2048
NA_HEADS_PER_STEP = 2
MM_BM = 1024
MM_BN = 1024
MERGE_BM = 1024
MERGE_BN = 512
OUT_BM = 1024
OUT_BN = 512
NA_QROWS = 4
NA_KROWS = 12
MEM_QBLOCK = 512

_NT_DIMS = (((1,), (1,)), ((), ()))


def _params(*semantics):
    return pltpu.CompilerParams(dimension_semantics=semantics, vmem_limit_bytes=VMEM_LIMIT_BYTES)


def _weighted_sum_and_total(e, v):
    hd = v.shape[1]
    r = jnp.dot(e, jnp.concatenate([v, jnp.ones_like(v)], axis=1), preferred_element_type=jnp.float32)
    return r[:, :hd], r[:, hd:]


def _sigmoid(x):
    return 0.5 * jnp.tanh(0.5 * x) + 0.5


def _rmsnorm_kernel(x_ref, g_ref, o_ref):
    x = x_ref[...]
    ms = jnp.mean(x * x, axis=-1, keepdims=True)
    o_ref[...] = ((x * lax.rsqrt(ms + RMS_EPS)) * g_ref[...]).astype(o_ref.dtype)


def _rmsnorm(x2d, gain):
    rows, d = x2d.shape
    return pl.pallas_call(
        _rmsnorm_kernel,
        grid=(rows // NORM_ROWS,),
        in_specs=[pl.BlockSpec((NORM_ROWS, d), lambda i: (i, 0)),
                  pl.BlockSpec((1, d), lambda i: (0, 0))],
        out_specs=pl.BlockSpec((NORM_ROWS, d), lambda i: (i, 0)),
        out_shape=jax.ShapeDtypeStruct((rows, d), jnp.bfloat16),
        compiler_params=_params("parallel"),
        name="rmsnorm",
    )(x2d, gain.reshape(1, d))


def _matmul_kernel(*refs, has_row_ssq, n_casts):
    x_ref, w_ref, s_ref = refs[:3]
    n_in = 3 + has_row_ssq + n_casts
    o_ref = refs[n_in]
    acc = jnp.dot(x_ref[...], w_ref[...], preferred_element_type=jnp.float32)
    if has_row_ssq:
        acc = acc * lax.rsqrt(refs[3][...] * (1.0 / x_ref.shape[1]) + RMS_EPS)
    o_ref[...] = (acc * s_ref[...]).astype(o_ref.dtype)
    for c in range(n_casts):
        dst = refs[n_in + 1 + c]
        dst[...] = refs[3 + has_row_ssq + c][...].astype(dst.dtype)


def _matmul(x2d, w_all, layer, col_scale, row_ssq=None, cast_src=None, cast_layer=None, cast_all=()):
    m, k = x2d.shape
    n = w_all.shape[2]
    bm = min(MM_BM, m)
    bn = min(MM_BN, n)
    assert m % bm == 0 and n % bn == 0
    grid = (m // bm, n // bn)
    operands = [x2d, w_all, col_scale]
    in_specs = [pl.BlockSpec((bm, k), lambda i, j: (i, 0)),
                pl.BlockSpec((None, k, bn), lambda i, j: (layer, 0, j)),
                pl.BlockSpec((1, bn), lambda i, j: (0, j))]
    out_specs = [pl.BlockSpec((bm, bn), lambda i, j: (i, j))]
    out_shape = [jax.ShapeDtypeStruct((m, n), jnp.bfloat16)]
    if row_ssq is not None:
        operands.append(row_ssq)
        in_specs.append(pl.BlockSpec((bm, 1), lambda i, j: (i, 0)))
    if cast_src is not None:
        _, k2, n2 = cast_src.shape
        assert k2 % grid[0] == 0 and n2 % grid[1] == 0
        cb = (k2 // grid[0], n2 // grid[1])
        operands.append(cast_src)
        in_specs.append(pl.BlockSpec((None,) + cb, lambda i, j: (cast_layer, i, j)))
        out_specs.append(pl.BlockSpec((None,) + cb, lambda i, j: (0, i, j)))
        out_shape.append(jax.ShapeDtypeStruct((1, k2, n2), jnp.bfloat16))
    for src in cast_all:
        depth, r, c = src.shape
        col_split = math.gcd(grid[1], c // LANES)
        lay_split = grid[1] // col_split
        assert r % (grid[0] * BF16_SUBLANES) == 0 and lay_split >= depth
        blk = (None, r // grid[0], c // col_split)

        def block_index(i, j, lay_split=lay_split, depth=depth):
            return (jnp.minimum(j % lay_split, depth - 1), i, j // lay_split)

        operands.append(src)
        in_specs.append(pl.BlockSpec(blk, block_index))
        out_specs.append(pl.BlockSpec(blk, block_index))
        out_shape.append(jax.ShapeDtypeStruct(src.shape, jnp.bfloat16))
    n_casts = (cast_src is not None) + len(cast_all)
    outs = pl.pallas_call(
        functools.partial(_matmul_kernel, has_row_ssq=row_ssq is not None, n_casts=n_casts),
        grid=grid, in_specs=in_specs, out_specs=out_specs, out_shape=out_shape,
        compiler_params=_params("parallel", "arbitrary"), name="proj_matmul",
    )(*operands)
    return outs[0] if n_casts == 0 else tuple(outs)


def _cast_kernel(src_ref, dst_ref):
    dst_ref[...] = src_ref[...].astype(dst_ref.dtype)


def _cast_layer(w_all, layer):
    _, k, n = w_all.shape
    bk, bn = min(CAST_ROWS, k), min(CAST_COLS, n)
    assert k % bk == 0 and n % bn == 0
    return pl.pallas_call(
        _cast_kernel,
        grid=(k // bk, n // bn),
        in_specs=[pl.BlockSpec((None, bk, bn), lambda i, j: (layer, i, j))],
        out_specs=pl.BlockSpec((None, bk, bn), lambda i, j: (0, i, j)),
        out_shape=jax.ShapeDtypeStruct((1, k, n), jnp.bfloat16),
        compiler_params=_params("parallel", "parallel"),
        name="cast_weights",
    )(w_all)


def _na_kernel(q_ref, k_ref, v_ref, z_ref, bias_ref, o_ref):
    seq = q_ref.shape[0]
    qb = NA_QROWS * GRID_W
    kb = NA_KROWS * GRID_W
    nblk = seq // qb
    grid_rows = seq // GRID_W

    for hh in range(NA_HEADS_PER_STEP):
        lanes = slice(hh * HEAD_DIM, (hh + 1) * HEAD_DIM)
        for i in range(nblk):
            case = 0 if i == 0 else (2 if i == nblk - 1 else 1)
            q0 = i * qb
            k0 = min(max(NA_QROWS * i - NA_ROWS // 2, 0), grid_rows - NA_KROWS) * GRID_W
            q = q_ref[q0:q0 + qb, lanes]
            k = k_ref[k0:k0 + kb, lanes]
            v = v_ref[k0:k0 + kb, lanes]
            s = lax.dot_general(q, k, _NT_DIMS, preferred_element_type=jnp.float32) + bias_ref[hh, case]
            mx = jnp.max(s, axis=-1, keepdims=True)
            num, tot = _weighted_sum_and_total(jnp.exp2(s - mx).astype(v.dtype), v)
            o = num / tot
            z = z_ref[q0:q0 + qb, lanes].astype(jnp.float32)
            o_ref[q0:q0 + qb, lanes] = (o * (z * _sigmoid(z))).astype(o_ref.dtype)


def _na_attention(proj, bias_all, layer):
    bsz, seq, _ = proj.shape
    qb = NA_QROWS * GRID_W
    kb = NA_KROWS * GRID_W

    hps = NA_HEADS_PER_STEP
    width = hps * HEAD_DIM
    assert NA_HEADS % hps == 0 and all(off % width == 0 for off in (OFF_QA, OFF_KA, OFF_VA, OFF_ZA))

    def col(off):
        return pl.BlockSpec((None, seq, width), lambda h, b: (b, 0, off // width + h))

    return pl.pallas_call(
        _na_kernel,
        grid=(NA_HEADS // hps, bsz),
        in_specs=[col(OFF_QA), col(OFF_KA), col(OFF_VA), col(OFF_ZA),
                  pl.BlockSpec((None, hps, 3, qb, kb), lambda h, b: (layer, h, 0, 0, 0))],
        out_specs=pl.BlockSpec((None, seq, width), lambda h, b: (b, 0, h)),
        out_shape=jax.ShapeDtypeStruct((bsz, seq, NA_WIDTH), jnp.bfloat16),
        compiler_params=_params("parallel", "arbitrary"),
        name="na_attention",
    )(proj, proj, proj, proj, bias_all)


def _na_bias_tables(rpb, seq):
    grid_rows = seq // GRID_W
    nblk = grid_rows // NA_QROWS
    win_rows = min(NA_ROWS, grid_rows)
    qc = np.arange(GRID_W)
    cstart = np.clip(qc - NA_COLS // 2, 0, GRID_W - NA_COLS)
    kc = np.arange(GRID_W)
    col_ok = (kc[None, :] >= cstart[:, None]) & (kc[None, :] < cstart[:, None] + NA_COLS)
    dc = np.clip(kc[None, :] - qc[:, None], -(NA_COLS - 1), NA_COLS - 1) + NA_COLS - 1
    row_ok, dr = [], []
    for i in (0, 1, nblk - 1):
        qr = NA_QROWS * i + np.arange(NA_QROWS)
        kr = np.clip(NA_QROWS * i - NA_ROWS // 2, 0, grid_rows - NA_KROWS) + np.arange(NA_KROWS)
        rs = np.clip(qr - win_rows // 2, 0, grid_rows - win_rows)
        ok = (kr[None, :] >= rs[:, None]) & (kr[None, :] < rs[:, None] + win_rows)
        row_ok.append(ok)
        dr.append(np.clip(kr[None, :] - qr[:, None] + NA_ROWS - 1, 0, 2 * NA_ROWS - 2))
    row_ok = np.stack(row_ok)
    dr = np.stack(dr)
    nd, ne = 2 * NA_ROWS - 1, 2 * NA_COLS - 1
    row_hot = (dr[..., None] == np.arange(nd)).astype(np.float32)
    col_hot = (dc[..., None] == np.arange(ne)).astype(np.float32)
    pair = LANES // GRID_W
    pair_hot = np.zeros((pair, ne, GRID_W, pair, GRID_W), np.float32)
    for b in range(pair):
        pair_hot[b, :, :, b, :] = np.transpose(col_hot, (2, 0, 1))
    pair_hot = pair_hot.reshape(pair * ne, GRID_W, LANES)
    mask = row_ok[:, :, None, :, None] & col_ok[None, None, :, None, :]
    mask_add = np.where(mask, 0.0, NEG_INF).astype(np.float32).reshape(
        3, NA_QROWS * GRID_W, NA_KROWS * GRID_W)
    nl, nh = rpb.shape[0], rpb.shape[1]
    g = jnp.einsum("sabd,lhde->lhsabe", jnp.asarray(row_hot), rpb.astype(jnp.float32),
                   precision=lax.Precision.HIGHEST)
    g = g.reshape(nl, nh, 3, NA_QROWS, NA_KROWS // pair, pair * ne)
    t = jnp.einsum("lhsapk,kxz->lhsaxpz", g, jnp.asarray(pair_hot), precision=lax.Precision.HIGHEST)
    t = t.reshape(nl, nh, 3, NA_QROWS * GRID_W, NA_KROWS * GRID_W)
    return t * LOG2E + jnp.asarray(mask_add)[None, None]


def _sw_kernel(q_ref, k_ref, v_ref, z0_ref, z1_ref, z2_ref, bias_ref, sink_ref, o_ref):
    z_refs = (z0_ref, z1_ref, z2_ref)
    seq = k_ref.shape[0]
    blk = SW_BLOCK
    nblk = seq // blk
    sk = sink_ref[...]

    for i in range(nblk):
        r0 = i * blk
        lo = max(i - 1, 0) * blk
        hi = min(i + 2, nblk) * blk
        c0 = lo - (i - 1) * blk
        q = q_ref[r0:r0 + blk, :]
        qs = jnp.concatenate([q[:, g * HEAD_DIM:(g + 1) * HEAD_DIM] for g in range(SW_GROUP)], axis=0)
        kw = k_ref[lo:hi, :]
        vw = v_ref[lo:hi, :]
        s = (lax.dot_general(qs, kw, _NT_DIMS, preferred_element_type=jnp.float32)
             + bias_ref[:, c0:c0 + hi - lo])
        mx = jnp.maximum(jnp.max(s, axis=-1, keepdims=True), sk)
        mx_wide = jnp.concatenate([mx] * ((hi - lo) // HEAD_DIM), axis=1)
        num, tot = _weighted_sum_and_total(jnp.exp2(s - mx_wide).astype(vw.dtype), vw)
        o = num / (tot + jnp.exp2(sk - mx))
        og =jnp.concatenate([o[g * blk:(g + 1) * blk, :] for g in range(SW_GROUP)], axis=1)
        z = jnp.concatenate([r[r0:r0 + blk, :] for r in z_refs], axis=1).astype(jnp.float32)
        o_ref[r0:r0 + blk, :] = (og * (z * _sigmoid(z))).astype(o_ref.dtype)


def _sw_attention(proj, bias, sink_col):
    bsz, seq, _ = proj.shape
    gw = SW_GROUP * HEAD_DIM

    def wide(off):
        return pl.BlockSpec((None, seq, gw), lambda k, b: (b, 0, off // gw + k))

    def narrow(off):
        return pl.BlockSpec((None, seq, HEAD_DIM), lambda k, b: (b, 0, off // HEAD_DIM + k))

    def gate(g):
        return pl.BlockSpec((None, seq, HEAD_DIM),
                            lambda k, b: (b, 0, OFF_ZB // HEAD_DIM + k * SW_GROUP + g))

    assert OFF_QB % gw == 0 and SW_GROUP == 3
    return pl.pallas_call(
        _sw_kernel,
        grid=(SW_KV_HEADS, bsz),
        in_specs=[wide(OFF_QB), narrow(OFF_KB), narrow(OFF_VB), gate(0), gate(1), gate(2),
                  pl.BlockSpec((SW_GROUP * SW_BLOCK, 3 * SW_BLOCK), lambda k, b: (k, 0)),
                  pl.BlockSpec((SW_GROUP * SW_BLOCK, HEAD_DIM), lambda k, b: (k, 0))],
        out_specs=pl.BlockSpec((None, seq, gw), lambda k, b: (b, 0, k)),
        out_shape=jax.ShapeDtypeStruct((bsz, seq, SW_WIDTH), jnp.bfloat16),
        compiler_params=_params("parallel", "arbitrary"),
        name="sw_attention",
    )(proj, proj, proj, proj, proj, proj, bias, sink_col)


def _t5_bucket_index(rel):
    nb = T5_BUCKETS // 2
    max_exact = nb // 2
    ret = (rel > 0).astype(np.int32) * nb
    n = np.abs(rel)
    large = max_exact + (np.log(np.maximum(n, 1) / max_exact)
                         / np.log(T5_MAX_DIST / max_exact) * (nb - max_exact)).astype(np.int32)
    large = np.minimum(large, nb - 1)
    return (ret + np.where(n < max_exact, n, large)).astype(np.int32)


def _sw_bias_table(t5_bias):
    rel = np.arange(3 * SW_BLOCK)[None, :] - SW_BLOCK - np.arange(SW_BLOCK)[:, None]
    band = np.abs(rel) <= SW_WINDOW
    hot = (_t5_bucket_index(rel)[..., None] == np.arange(T5_BUCKETS)).astype(np.float32)
    t = jnp.einsum("qkb,bh->hqk", jnp.asarray(hot), t5_bias.astype(jnp.float32),
                   precision=lax.Precision.HIGHEST)
    t = jnp.where(jnp.asarray(band)[None], t * LOG2E, NEG_INF)
    return t.reshape(SW_HEADS * SW_BLOCK, 3 * SW_BLOCK)


def _mem_kernel(q_ref, z_ref, k_ref, v_ref, o_ref):
    seq = q_ref.shape[0]
    k = k_ref[...]
    v = v_ref[...]

    for i in range(seq // MEM_QBLOCK):
        r0 = i * MEM_QBLOCK
        q = q_ref[r0:r0 + MEM_QBLOCK, :]
        s = lax.dot_general(q, k, _NT_DIMS, preferred_element_type=jnp.float32)
        mx = jnp.max(s, axis=-1, keepdims=True)
        e = jnp.exp2(s - mx)
        den = jnp.sum(e, axis=-1, keepdims=True)
        o = jnp.dot(e.astype(v.dtype), v, preferred_element_type=jnp.float32) / den
        z = z_ref[r0:r0 + MEM_QBLOCK, :].astype(jnp.float32)
        o_ref[r0:r0 + MEM_QBLOCK, :] = (o * (z * _sigmoid(z))).astype(o_ref.dtype)


def _mem_attention(proj, memkv):
    bsz, seq, _ = proj.shape
    mtok = memkv.shape[1]

    def qcol(off):
        return pl.BlockSpec((None, seq, MEM_HEAD_DIM), lambda h, b: (b, 0, off // MEM_HEAD_DIM + h))

    def kvcol(off):
        return pl.BlockSpec((None, mtok, MEM_HEAD_DIM), lambda h, b: (b, 0, off // MEM_HEAD_DIM + h))

    assert OFF_QM % MEM_HEAD_DIM == 0 and OFF_ZM % MEM_HEAD_DIM == 0
    return pl.pallas_call(
        _mem_kernel,
        grid=(MEM_HEADS, bsz),
        in_specs=[qcol(OFF_QM), qcol(OFF_ZM), kvcol(0), kvcol(MEM_WIDTH)],
        out_specs=pl.BlockSpec((None, seq, MEM_HEAD_DIM), lambda h, b: (b, 0, h)),
        out_shape=jax.ShapeDtypeStruct((bsz, seq, MEM_WIDTH), jnp.bfloat16),
        compiler_params=_params("parallel", "arbitrary"),
        name="mem_attention",
    )(proj, proj, memkv, memkv)


def _merge_kernel(a_ref, b_ref, m_ref, wa_ref, wb_ref, wm_ref, ga_ref, gb_ref, gm_ref, o_ref):
    def branch(x_ref, w_ref, g_ref):
        y = jnp.dot(x_ref[...], w_ref[...], preferred_element_type=jnp.float32)
        return _sigmoid(g_ref[...].astype(jnp.float32)) * y

    acc = branch(a_ref, wa_ref, ga_ref)
    acc = acc + branch(b_ref, wb_ref, gb_ref)
    acc = acc + branch(m_ref, wm_ref, gm_ref)
    o_ref[...] = acc.astype(o_ref.dtype)


def _merge(a2d, b2d, m2d, wa, wb, wm, layer, proj2d):
    rows = a2d.shape[0]
    bm = min(MERGE_BM, rows)
    bn = MERGE_BN

    def act(width):
        return pl.BlockSpec((bm, width), lambda i, j: (i, 0))

    def wgt(width):
        return pl.BlockSpec((None, width, bn), lambda i, j: (layer, 0, j))

    def gate(off):
        return pl.BlockSpec((bm, bn), lambda i, j: (i, off // bn + j))

    assert OFF_GA % bn == 0 and OFF_GB % bn == 0 and OFF_GM % bn == 0 and rows % bm == 0
    return pl.pallas_call(
        _merge_kernel,
        grid=(rows // bm, D_MODEL // bn),
        in_specs=[act(NA_WIDTH), act(SW_WIDTH), act(MEM_WIDTH),
                  wgt(NA_WIDTH), wgt(SW_WIDTH), wgt(MEM_WIDTH),
                  gate(OFF_GA), gate(OFF_GB), gate(OFF_GM)],
        out_specs=pl.BlockSpec((bm, bn), lambda i, j: (i, j)),
        out_shape=jax.ShapeDtypeStruct((rows, D_MODEL), jnp.bfloat16),
        compiler_params=_params("parallel", "arbitrary"),
        name="branch_merge",
    )(a2d, b2d, m2d, wa, wb, wm, proj2d, proj2d, proj2d)


def _outproj_kernel(*refs, emit_next):
    if emit_next:
        mg_ref, w_ref, x_ref, g_ref, gn_ref, o_ref, xg_ref, ssqn_ref, y_ref, ssq_ref, ssqn_acc = refs
    else:
        mg_ref, w_ref, x_ref, g_ref, o_ref, y_ref, ssq_ref = refs
    i = pl.program_id(0)
    j = pl.program_id(1)
    n_tiles = pl.num_programs(0) - 1
    bn = w_ref.shape[1]
    d = y_ref.shape[1]
    col = pl.multiple_of(j * bn, bn)
    cur = lax.rem(i, 2)

    def finish():
        inv = lax.rsqrt(ssq_ref[1 - cur] * (1.0 / d) + RMS_EPS)
        x_new = x_ref[...] + (y_ref[:, pl.ds(col, bn)] * inv) * g_ref[...]
        o_ref[...] = x_new
        if emit_next:
            xg_ref[...] = (x_new * gn_ref[...]).astype(xg_ref.dtype)
            total = jnp.sum(x_new * x_new, axis=-1, keepdims=True) + jnp.where(j == 0, 0.0, ssqn_acc[...])
            ssqn_acc[...] = total
            ssqn_ref[...] = total

    def project():
        y = jnp.dot(mg_ref[...], w_ref[...], preferred_element_type=jnp.float32)
        y_ref[:, pl.ds(col, bn)] = y
        part = jnp.sum(y * y, axis=-1, keepdims=True)
        ssq_ref[cur] = part + jnp.where(j == 0, 0.0, ssq_ref[cur])

    @pl.when(i == 0)
    def _():
        @pl.when(j == 0)
        def _():
            ssq_ref[...] = jnp.zeros_like(ssq_ref)
            if emit_next:
                ssqn_acc[...] = jnp.zeros_like(ssqn_acc)
        project()

    @pl.when((i > 0) & (i < n_tiles))
    def _():
        finish()
        project()

    @pl.when(i == n_tiles)
    def _():
        finish()


def _outproj(merged2d, w_all, layer, x2d, gain, next_gain=None):
    rows, d = x2d.shape
    bm = min(OUT_BM, rows)
    bn = OUT_BN
    n_tiles = rows // bm

    def finished_block(i, j):
        return (jnp.maximum(i - 1, 0), jnp.where(i == 0, 0, j))

    emit_next = next_gain is not None
    operands = [merged2d, w_all, x2d, gain.reshape(1, d)]
    in_specs = [pl.BlockSpec((bm, d), lambda i, j: (jnp.minimum(i, n_tiles - 1), 0)),
                pl.BlockSpec((None, d, bn), lambda i, j: (layer, 0, j)),
                pl.BlockSpec((bm, bn), finished_block),
                pl.BlockSpec((1, bn), lambda i, j: (0, j))]
    out_specs = [pl.BlockSpec((bm, bn), finished_block)]
    out_shape = [jax.ShapeDtypeStruct((rows, d), jnp.float32)]
    if emit_next:
        operands.append(next_gain.reshape(1, d))
        in_specs.append(pl.BlockSpec((1, bn), lambda i, j: (0, j)))
        out_specs += [pl.BlockSpec((bm, bn), finished_block),
                      pl.BlockSpec((bm, 1), lambda i, j: (jnp.maximum(i - 1, 0), 0))]
        out_shape += [jax.ShapeDtypeStruct((rows, d), jnp.bfloat16),
                      jax.ShapeDtypeStruct((rows, 1), jnp.float32)]
    outs = pl.pallas_call(
        functools.partial(_outproj_kernel, emit_next=emit_next),
        grid=(n_tiles + 1, d // bn),
        in_specs=in_specs, out_specs=out_specs, out_shape=out_shape,
        scratch_shapes=[pltpu.VMEM((bm, d), jnp.float32),
                        pltpu.VMEM((2, bm, 1), jnp.float32)]
        + ([pltpu.VMEM((bm, 1), jnp.float32)] if emit_next else []),
        compiler_params=_params("arbitrary", "arbitrary"),
        name="out_proj_norm",
    )(*operands)
    return tuple(outs) if emit_next else outs[0]


_SIDE_CAST_WEIGHTS = ("w_mem_kv", "w_branch_a", "w_branch_b", "w_branch_m", "w_out")


def _in_proj_scale():
    s = np.ones((1, IN_WIDTH), np.float32)
    s[0, OFF_QA:OFF_QA + NA_WIDTH] = HEAD_DIM ** -0.5 * LOG2E
    s[0, OFF_QB:OFF_QB + SW_WIDTH] = HEAD_DIM ** -0.5 * LOG2E
    s[0, OFF_QM:OFF_QM + MEM_WIDTH] = MEM_HEAD_DIM ** -0.5 * LOG2E
    return s


def _trunk(x, mem, p):
    bsz, seq, d = x.shape
    mtok = mem.shape[1]
    x2d = x.reshape(bsz * seq, d)
    mem2d = mem.reshape(bsz * mtok, d)
    h, h_ssq = _rmsnorm(x2d, p["pre_norm"][0]), None
    for l in range(DEPTH):
        w_in_bf = p["w_in_bf16"]
        pending = [name for name in _SIDE_CAST_WEIGHTS if p[name].dtype != jnp.bfloat16]
        next_w_in = l + 1 < DEPTH and w_in_bf[l + 1] is None
        outs = _matmul(h, w_in_bf[l], 0, p["in_scale"], row_ssq=h_ssq,
                       cast_src=p["w_in"] if next_w_in else None, cast_layer=l + 1,
                       cast_all=[p[name] for name in pending])
        if next_w_in or pending:
            proj2d, copies = outs[0], list(outs[1:])
            if next_w_in:
                w_in_bf[l + 1] = copies.pop(0)
            p.update(zip(pending, copies))
        else:
            proj2d = outs
        proj = proj2d.reshape(bsz, seq, IN_WIDTH)
        mem_h = _rmsnorm(mem2d, p["mem_norm"][l])
        memkv = _matmul(mem_h, p["w_mem_kv"], l, p["kv_scale"]).reshape(bsz, mtok, 2 * MEM_WIDTH)
        a = _na_attention(proj, p["na_bias"], l)
        b = _sw_attention(proj, p["sw_bias"], p["sink_col"][l])
        m = _mem_attention(proj, memkv)
        merged = _merge(a.reshape(bsz * seq, NA_WIDTH), b.reshape(bsz * seq, SW_WIDTH),
                        m.reshape(bsz * seq, MEM_WIDTH), p["w_branch_a"], p["w_branch_b"],
                        p["w_branch_m"], l, proj2d)
        if l + 1 < DEPTH:
            x2d, h, h_ssq = _outproj(merged, p["w_out"], l, x2d, p["post_norm"][l],
                                     next_gain=p["pre_norm"][l + 1])
        else:
            x2d = _outproj(merged, p["w_out"], l, x2d, p["post_norm"][l])
    return x2d.reshape(bsz, seq, d)


def kernel(x_prompt, x_sample, mem_prompt, mem_sample, pre_norm, post_norm, mem_norm, w_in, w_mem_kv,
           w_branch_a, w_branch_b, w_branch_m, w_out, na_rpb, attn_sink, t5_bias):
    bf16 = jnp.bfloat16
    seq = x_prompt.shape[1]
    assert x_sample.shape[1] == seq and seq % (NA_QROWS * GRID_W) == 0 and seq % SW_BLOCK == 0
    assert seq // GRID_W >= NA_KROWS and seq % MEM_QBLOCK == 0 and seq // SW_BLOCK >= 2
    assert LANES % GRID_W == 0 and NA_KROWS % (LANES // GRID_W) == 0
    sink_col = jnp.repeat(attn_sink.astype(jnp.float32) * LOG2E, SW_BLOCK, axis=1)
    sink_col = jnp.broadcast_to(sink_col[..., None], sink_col.shape + (HEAD_DIM,))
    p = {
        "pre_norm": pre_norm, "post_norm": post_norm, "mem_norm": mem_norm,
        "w_in": w_in, "w_in_bf16": [_cast_layer(w_in, 0)] + [None] * (DEPTH - 1),
        "w_mem_kv": w_mem_kv, "w_branch_a": w_branch_a, "w_branch_b": w_branch_b,
        "w_branch_m": w_branch_m, "w_out": w_out,
        "in_scale": jnp.asarray(_in_proj_scale()),
        "kv_scale": jnp.ones((1, 2 * MEM_WIDTH), jnp.float32),
        "na_bias": _na_bias_tables(na_rpb, seq),
        "sw_bias": _sw_bias_table(t5_bias),
        "sink_col": sink_col,
    }
    y_prompt = _trunk(x_prompt, mem_prompt, p)
    y_sample = _trunk(x_sample, mem_sample, p)
    return (y_prompt, y_sample)
```

```python
import functools
import math

import numpy as np
import jax
import jax.numpy as jnp
from jax import lax
from jax.experimental import pallas as pl
from jax.experimental.pallas import tpu as pltpu

D_MODEL = 4096
DEPTH = 2
GRID_W = 64
HEAD_DIM = 128
NA_HEADS = 12
NA_ROWS = 8
NA_COLS = 16
SW_HEADS = 12
SW_KV_HEADS = 4
SW_GROUP = SW_HEADS // SW_KV_HEADS
SW_WINDOW = 128
SW_BLOCK = 128
T5_BUCKETS = 32
T5_MAX_DIST = 128
MEM_HEADS = 4
MEM_HEAD_DIM = 256
NA_WIDTH = NA_HEADS * HEAD_DIM
SW_WIDTH = SW_HEADS * HEAD_DIM
SW_KV_WIDTH = SW_KV_HEADS * HEAD_DIM
MEM_WIDTH = MEM_HEADS * MEM_HEAD_DIM
IN_SPLITS = (NA_WIDTH, NA_WIDTH, NA_WIDTH, NA_WIDTH,
             SW_WIDTH, SW_KV_WIDTH, SW_KV_WIDTH, SW_WIDTH,
             MEM_WIDTH, MEM_WIDTH,
             D_MODEL, D_MODEL, D_MODEL)
IN_WIDTH = sum(IN_SPLITS)
(OFF_QA, OFF_KA, OFF_VA, OFF_ZA, OFF_QB, OFF_KB, OFF_VB, OFF_ZB,
 OFF_QM, OFF_ZM, OFF_GA, OFF_GB, OFF_GM) = [int(c) for c in np.cumsum((0,) + IN_SPLITS[:-1])]
RMS_EPS = 1e-6
NEG_INF = -1e30
LOG2E = float(np.log2(np.e))

LANES = 128
BF16_SUBLANES = 16
VMEM_LIMIT_BYTES = 60 * 1024 * 1024
NORM_ROWS = 512
CAST_ROWS = 512
CAST_COLS = 2048
NA_HEADS_PER_STEP = 4
MM_BM = 1024
MM_BN = 1024
MERGE_BM = 1024
MERGE_BN = 1024
OUT_BM = 1024
OUT_BN = 512
NA_QROWS = 4
NA_KROWS = 12
MEM_QBLOCK = 512

_NT_DIMS = (((1,), (1,)), ((), ()))


def _params(*semantics):
    return pltpu.CompilerParams(dimension_semantics=semantics, vmem_limit_bytes=VMEM_LIMIT_BYTES)


def _weighted_sum_and_total(e, v):
    hd = v.shape[1]
    r = jnp.dot(e, jnp.concatenate([v, jnp.ones_like(v)], axis=1), preferred_element_type=jnp.float32)
    return r[:, :hd], r[:, hd:]


def _sigmoid(x):
    return 0.5 * jnp.tanh(0.5 * x) + 0.5


def _rmsnorm_kernel(x_ref, g_ref, o_ref):
    x = x_ref[...]
    ms = jnp.mean(x * x, axis=-1, keepdims=True)
    o_ref[...] = ((x * lax.rsqrt(ms + RMS_EPS)) * g_ref[...]).astype(o_ref.dtype)


def _rmsnorm(x2d, gain):
    rows, d = x2d.shape
    return pl.pallas_call(
        _rmsnorm_kernel,
        grid=(rows // NORM_ROWS,),
        in_specs=[pl.BlockSpec((NORM_ROWS, d), lambda i: (i, 0)),
                  pl.BlockSpec((1, d), lambda i: (0, 0))],
        out_specs=pl.BlockSpec((NORM_ROWS, d), lambda i: (i, 0)),
        out_shape=jax.ShapeDtypeStruct((rows, d), jnp.bfloat16),
        compiler_params=_params("parallel"),
        name="rmsnorm",
    )(x2d, gain.reshape(1, d))


def _matmul_kernel(*refs, has_row_ssq, n_casts):
    x_ref, w_ref, s_ref = refs[:3]
    n_in = 3 + has_row_ssq + n_casts
    o_ref = refs[n_in]
    acc = jnp.dot(x_ref[...], w_ref[...], preferred_element_type=jnp.float32)
    if has_row_ssq:
        acc = acc * lax.rsqrt(refs[3][...] * (1.0 / x_ref.shape[1]) + RMS_EPS)
    o_ref[...] = (acc * s_ref[...]).astype(o_ref.dtype)
    for c in range(n_casts):
        dst = refs[n_in + 1 + c]
        dst[...] = refs[3 + has_row_ssq + c][...].astype(dst.dtype)


def _matmul(x2d, w_all, layer, col_scale, row_ssq=None, cast_src=None, cast_layer=None, cast_all=()):
    m, k = x2d.shape
    n = w_all.shape[2]
    bm = min(MM_BM, m)
    bn = min(MM_BN, n)
    assert m % bm == 0 and n % bn == 0
    grid = (m // bm, n // bn)
    operands = [x2d, w_all, col_scale]
    in_specs = [pl.BlockSpec((bm, k), lambda i, j: (i, 0)),
                pl.BlockSpec((None, k, bn), lambda i, j: (layer, 0, j)),
                pl.BlockSpec((1, bn), lambda i, j: (0, j))]
    out_specs = [pl.BlockSpec((bm, bn), lambda i, j: (i, j))]
    out_shape = [jax.ShapeDtypeStruct((m, n), jnp.bfloat16)]
    if row_ssq is not None:
        operands.append(row_ssq)
        in_specs.append(pl.BlockSpec((bm, 1), lambda i, j: (i, 0)))
    if cast_src is not None:
        _, k2, n2 = cast_src.shape
        assert k2 % grid[0] == 0 and n2 % grid[1] == 0
        cb = (k2 // grid[0], n2 // grid[1])
        operands.append(cast_src)
        in_specs.append(pl.BlockSpec((None,) + cb, lambda i, j: (cast_layer, i, j)))
        out_specs.append(pl.BlockSpec((None,) + cb, lambda i, j: (0, i, j)))
        out_shape.append(jax.ShapeDtypeStruct((1, k2, n2), jnp.bfloat16))
    for src in cast_all:
        depth, r, c = src.shape
        col_split = math.gcd(grid[1], c // LANES)
        lay_split = grid[1] // col_split
        assert r % (grid[0] * BF16_SUBLANES) == 0 and lay_split >= depth
        blk = (None, r // grid[0], c // col_split)

        def block_index(i, j, lay_split=lay_split, depth=depth):
            return (jnp.minimum(j % lay_split, depth - 1), i, j // lay_split)

        operands.append(src)
        in_specs.append(pl.BlockSpec(blk, block_index))
        out_specs.append(pl.BlockSpec(blk, block_index))
        out_shape.append(jax.ShapeDtypeStruct(src.shape, jnp.bfloat16))
    n_casts = (cast_src is not None) + len(cast_all)
    outs = pl.pallas_call(
        functools.partial(_matmul_kernel, has_row_ssq=row_ssq is not None, n_casts=n_casts),
        grid=grid, in_specs=in_specs, out_specs=out_specs, out_shape=out_shape,
        compiler_params=_params("parallel", "arbitrary"), name="proj_matmul",
    )(*operands)
    return outs[0] if n_casts == 0 else tuple(outs)


def _cast_kernel(src_ref, dst_ref):
    dst_ref[...] = src_ref[...].astype(dst_ref.dtype)


def _cast_layer(w_all, layer):
    _, k, n = w_all.shape
    bk, bn = min(CAST_ROWS, k), min(CAST_COLS, n)
    assert k % bk == 0 and n % bn == 0
    return pl.pallas_call(
        _cast_kernel,
        grid=(k // bk, n // bn),
        in_specs=[pl.BlockSpec((None, bk, bn), lambda i, j: (layer, i, j))],
        out_specs=pl.BlockSpec((None, bk, bn), lambda i, j: (0, i, j)),
        out_shape=jax.ShapeDtypeStruct((1, k, n), jnp.bfloat16),
        compiler_params=_params("parallel", "parallel"),
        name="cast_weights",
    )(w_all)


def _na_kernel(q_ref, k_ref, v_ref, z_ref, bias_ref, o_ref):
    seq = q_ref.shape[0]
    qb = NA_QROWS * GRID_W
    kb = NA_KROWS * GRID_W
    nblk = seq // qb
    grid_rows = seq // GRID_W

    for hh in range(NA_HEADS_PER_STEP):
        lanes = slice(hh * HEAD_DIM, (hh + 1) * HEAD_DIM)
        for i in range(nblk):
            case = 0 if i == 0 else (2 if i == nblk - 1 else 1)
            q0 = i * qb
            k0 = min(max(NA_QROWS * i - NA_ROWS // 2, 0), grid_rows - NA_KROWS) * GRID_W
            q = q_ref[q0:q0 + qb, lanes]
            k = k_ref[k0:k0 + kb, lanes]
            v = v_ref[k0:k0 + kb, lanes]
            s = lax.dot_general(q, k, _NT_DIMS, preferred_element_type=jnp.float32) + bias_ref[hh, case]
            mx = jnp.max(s, axis=-1, keepdims=True)
            num, tot = _weighted_sum_and_total(jnp.exp2(s - mx).astype(v.dtype), v)
            o = num / tot
            z = z_ref[q0:q0 + qb, lanes].astype(jnp.float32)
            o_ref[q0:q0 + qb, lanes] = (o * (z * _sigmoid(z))).astype(o_ref.dtype)


def _na_attention(proj, bias_all, layer):
    bsz, seq, _ = proj.shape
    qb = NA_QROWS * GRID_W
    kb = NA_KROWS * GRID_W

    hps = NA_HEADS_PER_STEP
    width = hps * HEAD_DIM
    assert NA_HEADS % hps == 0 and all(off % width == 0 for off in (OFF_QA, OFF_KA, OFF_VA, OFF_ZA))

    def col(off):
        return pl.BlockSpec((None, seq, width), lambda h, b: (b, 0, off // width + h))

    return pl.pallas_call(
        _na_kernel,
        grid=(NA_HEADS // hps, bsz),
        in_specs=[col(OFF_QA), col(OFF_KA), col(OFF_VA), col(OFF_ZA),
                  pl.BlockSpec((None, hps, 3, qb, kb), lambda h, b: (layer, h, 0, 0, 0))],
        out_specs=pl.BlockSpec((None, seq, width), lambda h, b: (b, 0, h)),
        out_shape=jax.ShapeDtypeStruct((bsz, seq, NA_WIDTH), jnp.bfloat16),
        compiler_params=_params("parallel", "arbitrary"),
        name="na_attention",
    )(proj, proj, proj, proj, bias_all)


def _na_bias_tables(rpb, seq):
    grid_rows = seq // GRID_W
    nblk = grid_rows // NA_QROWS
    win_rows = min(NA_ROWS, grid_rows)
    qc = np.arange(GRID_W)
    cstart = np.clip(qc - NA_COLS // 2, 0, GRID_W - NA_COLS)
    kc = np.arange(GRID_W)
    col_ok = (kc[None, :] >= cstart[:, None]) & (kc[None, :] < cstart[:, None] + NA_COLS)
    dc = np.clip(kc[None, :] - qc[:, None], -(NA_COLS - 1), NA_COLS - 1) + NA_COLS - 1
    row_ok, dr = [], []
    for i in (0, 1, nblk - 1):
        qr = NA_QROWS * i + np.arange(NA_QROWS)
        kr = np.clip(NA_QROWS * i - NA_ROWS // 2, 0, grid_rows - NA_KROWS) + np.arange(NA_KROWS)
        rs = np.clip(qr - win_rows // 2, 0, grid_rows - win_rows)
        ok = (kr[None, :] >= rs[:, None]) & (kr[None, :] < rs[:, None] + win_rows)
        row_ok.append(ok)
        dr.append(np.clip(kr[None, :] - qr[:, None] + NA_ROWS - 1, 0, 2 * NA_ROWS - 2))
    row_ok = np.stack(row_ok)
    dr = np.stack(dr)
    nd, ne = 2 * NA_ROWS - 1, 2 * NA_COLS - 1
    row_hot = (dr[..., None] == np.arange(nd)).astype(np.float32)
    col_hot = (dc[..., None] == np.arange(ne)).astype(np.float32)
    pair = LANES // GRID_W
    pair_hot = np.zeros((pair, ne, GRID_W, pair, GRID_W), np.float32)
    for b in range(pair):
        pair_hot[b, :, :, b, :] = np.transpose(col_hot, (2, 0, 1))
    pair_hot = pair_hot.reshape(pair * ne, GRID_W, LANES)
    mask = row_ok[:, :, None, :, None] & col_ok[None, None, :, None, :]
    mask_add = np.where(mask, 0.0, NEG_INF).astype(np.float32).reshape(
        3, NA_QROWS * GRID_W, NA_KROWS * GRID_W)
    nl, nh = rpb.shape[0], rpb.shape[1]
    g = jnp.einsum("sabd,lhde->lhsabe", jnp.asarray(row_hot), rpb.astype(jnp.float32),
                   precision=lax.Precision.HIGHEST)
    g = g.reshape(nl, nh, 3, NA_QROWS, NA_KROWS // pair, pair * ne)
    t = jnp.einsum("lhsapk,kxz->lhsaxpz", g, jnp.asarray(pair_hot), precision=lax.Precision.HIGHEST)
    t = t.reshape(nl, nh, 3, NA_QROWS * GRID_W, NA_KROWS * GRID_W)
    return t * LOG2E + jnp.asarray(mask_add)[None, None]


def _sw_kernel(q_ref, k_ref, v_ref, z0_ref, z1_ref, z2_ref, bias_ref, sink_ref, o_ref):
    z_refs = (z0_ref, z1_ref, z2_ref)
    seq = k_ref.shape[0]
    blk = SW_BLOCK
    nblk = seq // blk
    sk = sink_ref[...]

    for i in range(nblk):
        r0 = i * blk
        lo = max(i - 1, 0) * blk
        hi = min(i + 2, nblk) * blk
        c0 = lo - (i - 1) * blk
        q = q_ref[r0:r0 + blk, :]
        qs = jnp.concatenate([q[:, g * HEAD_DIM:(g + 1) * HEAD_DIM] for g in range(SW_GROUP)], axis=0)
        kw = k_ref[lo:hi, :]
        vw = v_ref[lo:hi, :]
        s = (lax.dot_general(qs, kw, _NT_DIMS, preferred_element_type=jnp.float32)
             + bias_ref[:, c0:c0 + hi - lo])
        mx = jnp.maximum(jnp.max(s, axis=-1, keepdims=True), sk)
        mx_wide = jnp.concatenate([mx] * ((hi - lo) // HEAD_DIM), axis=1)
        num, tot = _weighted_sum_and_total(jnp.exp2(s - mx_wide).astype(vw.dtype), vw)
        o = num / (tot + jnp.exp2(sk - mx))
        og =jnp.concatenate([o[g * blk:(g + 1) * blk, :] for g in range(SW_GROUP)], axis=1)
        z = jnp.concatenate([r[r0:r0 + blk, :] for r in z_refs], axis=1).astype(jnp.float32)
        o_ref[r0:r0 + blk, :] = (og * (z * _sigmoid(z))).astype(o_ref.dtype)


def _sw_attention(proj, bias, sink_col):
    bsz, seq, _ = proj.shape
    gw = SW_GROUP * HEAD_DIM

    def wide(off):
        return pl.BlockSpec((None, seq, gw), lambda k, b: (b, 0, off // gw + k))

    def narrow(off):
        return pl.BlockSpec((None, seq, HEAD_DIM), lambda k, b: (b, 0, off // HEAD_DIM + k))

    def gate(g):
        return pl.BlockSpec((None, seq, HEAD_DIM),
                            lambda k, b: (b, 0, OFF_ZB // HEAD_DIM + k * SW_GROUP + g))

    assert OFF_QB % gw == 0 and SW_GROUP == 3
    return pl.pallas_call(
        _sw_kernel,
        grid=(SW_KV_HEADS, bsz),
        in_specs=[wide(OFF_QB), narrow(OFF_KB), narrow(OFF_VB), gate(0), gate(1), gate(2),
                  pl.BlockSpec((SW_GROUP * SW_BLOCK, 3 * SW_BLOCK), lambda k, b: (k, 0)),
                  pl.BlockSpec((SW_GROUP * SW_BLOCK, HEAD_DIM), lambda k, b: (k, 0))],
        out_specs=pl.BlockSpec((None, seq, gw), lambda k, b: (b, 0, k)),
        out_shape=jax.ShapeDtypeStruct((bsz, seq, SW_WIDTH), jnp.bfloat16),
        compiler_params=_params("parallel", "arbitrary"),
        name="sw_attention",
    )(proj, proj, proj, proj, proj, proj, bias, sink_col)


def _t5_bucket_index(rel):
    nb = T5_BUCKETS // 2
    max_exact = nb // 2
    ret = (rel > 0).astype(np.int32) * nb
    n = np.abs(rel)
    large = max_exact + (np.log(np.maximum(n, 1) / max_exact)
                         / np.log(T5_MAX_DIST / max_exact) * (nb - max_exact)).astype(np.int32)
    large = np.minimum(large, nb - 1)
    return (ret + np.where(n < max_exact, n, large)).astype(np.int32)


def _sw_bias_table(t5_bias):
    rel = np.arange(3 * SW_BLOCK)[None, :] - SW_BLOCK - np.arange(SW_BLOCK)[:, None]
    band = np.abs(rel) <= SW_WINDOW
    hot = (_t5_bucket_index(rel)[..., None] == np.arange(T5_BUCKETS)).astype(np.float32)
    t = jnp.einsum("qkb,bh->hqk", jnp.asarray(hot), t5_bias.astype(jnp.float32),
                   precision=lax.Precision.HIGHEST)
    t = jnp.where(jnp.asarray(band)[None], t * LOG2E, NEG_INF)
    return t.reshape(SW_HEADS * SW_BLOCK, 3 * SW_BLOCK)


def _mem_kernel(q_ref, z_ref, k_ref, v_ref, o_ref):
    seq = q_ref.shape[0]
    k = k_ref[...]
    v = v_ref[...]

    for i in range(seq // MEM_QBLOCK):
        r0 = i * MEM_QBLOCK
        q = q_ref[r0:r0 + MEM_QBLOCK, :]
        s = lax.dot_general(q, k, _NT_DIMS, preferred_element_type=jnp.float32)
        mx = jnp.max(s, axis=-1, keepdims=True)
        e = jnp.exp2(s - mx)
        den = jnp.sum(e, axis=-1, keepdims=True)
        o = jnp.dot(e.astype(v.dtype), v, preferred_element_type=jnp.float32) / den
        z = z_ref[r0:r0 + MEM_QBLOCK, :].astype(jnp.float32)
        o_ref[r0:r0 + MEM_QBLOCK, :] = (o * (z * _sigmoid(z))).astype(o_ref.dtype)


def _mem_attention(proj, memkv):
    bsz, seq, _ = proj.shape
    mtok = memkv.shape[1]

    def qcol(off):
        return pl.BlockSpec((None, seq, MEM_HEAD_DIM), lambda h, b: (b, 0, off // MEM_HEAD_DIM + h))

    def kvcol(off):
        return pl.BlockSpec((None, mtok, MEM_HEAD_DIM), lambda h, b: (b, 0, off // MEM_HEAD_DIM + h))

    assert OFF_QM % MEM_HEAD_DIM == 0 and OFF_ZM % MEM_HEAD_DIM == 0
    return pl.pallas_call(
        _mem_kernel,
        grid=(MEM_HEADS, bsz),
        in_specs=[qcol(OFF_QM), qcol(OFF_ZM), kvcol(0), kvcol(MEM_WIDTH)],
        out_specs=pl.BlockSpec((None, seq, MEM_HEAD_DIM), lambda h, b: (b, 0, h)),
        out_shape=jax.ShapeDtypeStruct((bsz, seq, MEM_WIDTH), jnp.bfloat16),
        compiler_params=_params("parallel", "arbitrary"),
        name="mem_attention",
    )(proj, proj, memkv, memkv)


def _merge_kernel(a_ref, b_ref, m_ref, wa_ref, wb_ref, wm_ref, ga_ref, gb_ref, gm_ref, o_ref):
    def branch(x_ref, w_ref, g_ref):
        y = jnp.dot(x_ref[...], w_ref[...], preferred_element_type=jnp.float32)
        return _sigmoid(g_ref[...].astype(jnp.float32)) * y

    acc = branch(a_ref, wa_ref, ga_ref)
    acc = acc + branch(b_ref, wb_ref, gb_ref)
    acc = acc + branch(m_ref, wm_ref, gm_ref)
    o_ref[...] = acc.astype(o_ref.dtype)


def _merge(a2d, b2d, m2d, wa, wb, wm, layer, proj2d):
    rows = a2d.shape[0]
    bm = min(MERGE_BM, rows)
    bn = MERGE_BN

    def act(width):
        return pl.BlockSpec((bm, width), lambda i, j: (i, 0))

    def wgt(width):
        return pl.BlockSpec((None, width, bn), lambda i, j: (layer, 0, j))

    def gate(off):
        return pl.BlockSpec((bm, bn), lambda i, j: (i, off // bn + j))

    assert OFF_GA % bn == 0 and OFF_GB % bn == 0 and OFF_GM % bn == 0 and rows % bm == 0
    return pl.pallas_call(
        _merge_kernel,
        grid=(rows // bm, D_MODEL // bn),
        in_specs=[act(NA_WIDTH), act(SW_WIDTH), act(MEM_WIDTH),
                  wgt(NA_WIDTH), wgt(SW_WIDTH), wgt(MEM_WIDTH),
                  gate(OFF_GA), gate(OFF_GB), gate(OFF_GM)],
        out_specs=pl.BlockSpec((bm, bn), lambda i, j: (i, j)),
        out_shape=jax.ShapeDtypeStruct((rows, D_MODEL), jnp.bfloat16),
        compiler_params=_params("parallel", "arbitrary"),
        name="branch_merge",
    )(a2d, b2d, m2d, wa, wb, wm, proj2d, proj2d, proj2d)


def _outproj_kernel(*refs, emit_next):
    if emit_next:
        mg_ref, w_ref, x_ref, g_ref, gn_ref, o_ref, xg_ref, ssqn_ref, y_ref, ssq_ref, ssqn_acc = refs
    else:
        mg_ref, w_ref, x_ref, g_ref, o_ref, y_ref, ssq_ref = refs
    i = pl.program_id(0)
    j = pl.program_id(1)
    n_tiles = pl.num_programs(0) - 1
    bn = w_ref.shape[1]
    d = y_ref.shape[1]
    col = pl.multiple_of(j * bn, bn)
    cur = lax.rem(i, 2)

    def finish():
        inv = lax.rsqrt(ssq_ref[1 - cur] * (1.0 / d) + RMS_EPS)
        x_new = x_ref[...] + (y_ref[:, pl.ds(col, bn)] * inv) * g_ref[...]
        o_ref[...] = x_new
        if emit_next:
            xg_ref[...] = (x_new * gn_ref[...]).astype(xg_ref.dtype)
            total = jnp.sum(x_new * x_new, axis=-1, keepdims=True) + jnp.where(j == 0, 0.0, ssqn_acc[...])
            ssqn_acc[...] = total
            ssqn_ref[...] = total

    def project():
        y = jnp.dot(mg_ref[...], w_ref[...], preferred_element_type=jnp.float32)
        y_ref[:, pl.ds(col, bn)] = y
        part = jnp.sum(y * y, axis=-1, keepdims=True)
        ssq_ref[cur] = part + jnp.where(j == 0, 0.0, ssq_ref[cur])

    @pl.when(i == 0)
    def _():
        @pl.when(j == 0)
        def _():
            ssq_ref[...] = jnp.zeros_like(ssq_ref)
            if emit_next:
                ssqn_acc[...] = jnp.zeros_like(ssqn_acc)
        project()

    @pl.when((i > 0) & (i < n_tiles))
    def _():
        finish()
        project()

    @pl.when(i == n_tiles)
    def _():
        finish()


def _outproj(merged2d, w_all, layer, x2d, gain, next_gain=None):
    rows, d = x2d.shape
    bm = min(OUT_BM, rows)
    bn = OUT_BN
    n_tiles = rows // bm

    def finished_block(i, j):
        return (jnp.maximum(i - 1, 0), jnp.where(i == 0, 0, j))

    emit_next = next_gain is not None
    operands = [merged2d, w_all, x2d, gain.reshape(1, d)]
    in_specs = [pl.BlockSpec((bm, d), lambda i, j: (jnp.minimum(i, n_tiles - 1), 0)),
                pl.BlockSpec((None, d, bn), lambda i, j: (layer, 0, j)),
                pl.BlockSpec((bm, bn), finished_block),
                pl.BlockSpec((1, bn), lambda i, j: (0, j))]
    out_specs = [pl.BlockSpec((bm, bn), finished_block)]
    out_shape = [jax.ShapeDtypeStruct((rows, d), jnp.float32)]
    if emit_next:
        operands.append(next_gain.reshape(1, d))
        in_specs.append(pl.BlockSpec((1, bn), lambda i, j: (0, j)))
        out_specs += [pl.BlockSpec((bm, bn), finished_block),
                      pl.BlockSpec((bm, 1), lambda i, j: (jnp.maximum(i - 1, 0), 0))]
        out_shape += [jax.ShapeDtypeStruct((rows, d), jnp.bfloat16),
                      jax.ShapeDtypeStruct((rows, 1), jnp.float32)]
    outs = pl.pallas_call(
        functools.partial(_outproj_kernel, emit_next=emit_next),
        grid=(n_tiles + 1, d // bn),
        in_specs=in_specs, out_specs=out_specs, out_shape=out_shape,
        scratch_shapes=[pltpu.VMEM((bm, d), jnp.float32),
                        pltpu.VMEM((2, bm, 1), jnp.float32)]
        + ([pltpu.VMEM((bm, 1), jnp.float32)] if emit_next else []),
        compiler_params=_params("arbitrary", "arbitrary"),
        name="out_proj_norm",
    )(*operands)
    return tuple(outs) if emit_next else outs[0]


_SIDE_CAST_WEIGHTS = ("w_mem_kv", "w_branch_a", "w_branch_b", "w_branch_m", "w_out")


def _in_proj_scale():
    s = np.ones((1, IN_WIDTH), np.float32)
    s[0, OFF_QA:OFF_QA + NA_WIDTH] = HEAD_DIM ** -0.5 * LOG2E
    s[0, OFF_QB:OFF_QB + SW_WIDTH] = HEAD_DIM ** -0.5 * LOG2E
    s[0, OFF_QM:OFF_QM + MEM_WIDTH] = MEM_HEAD_DIM ** -0.5 * LOG2E
    return s


def _trunk(x, mem, p):
    bsz, seq, d = x.shape
    mtok = mem.shape[1]
    x2d = x.reshape(bsz * seq, d)
    mem2d = mem.reshape(bsz * mtok, d)
    h, h_ssq = _rmsnorm(x2d, p["pre_norm"][0]), None
    for l in range(DEPTH):
        w_in_bf = p["w_in_bf16"]
        pending = [name for name in _SIDE_CAST_WEIGHTS if p[name].dtype != jnp.bfloat16]
        next_w_in = l + 1 < DEPTH and w_in_bf[l + 1] is None
        outs = _matmul(h, w_in_bf[l], 0, p["in_scale"], row_ssq=h_ssq,
                       cast_src=p["w_in"] if next_w_in else None, cast_layer=l + 1,
                       cast_all=[p[name] for name in pending])
        if next_w_in or pending:
            proj2d, copies = outs[0], list(outs[1:])
            if next_w_in:
                w_in_bf[l + 1] = copies.pop(0)
            p.update(zip(pending, copies))
        else:
            proj2d = outs
        proj = proj2d.reshape(bsz, seq, IN_WIDTH)
        mem_h = _rmsnorm(mem2d, p["mem_norm"][l])
        memkv = _matmul(mem_h, p["w_mem_kv"], l, p["kv_scale"]).reshape(bsz, mtok, 2 * MEM_WIDTH)
        a = _na_attention(proj, p["na_bias"], l)
        b = _sw_attention(proj, p["sw_bias"], p["sink_col"][l])
        m = _mem_attention(proj, memkv)
        merged = _merge(a.reshape(bsz * seq, NA_WIDTH), b.reshape(bsz * seq, SW_WIDTH),
                        m.reshape(bsz * seq, MEM_WIDTH), p["w_branch_a"], p["w_branch_b"],
                        p["w_branch_m"], l, proj2d)
        if l + 1 < DEPTH:
            x2d, h, h_ssq = _outproj(merged, p["w_out"], l, x2d, p["post_norm"][l],
                                     next_gain=p["pre_norm"][l + 1])
        else:
            x2d = _outproj(merged, p["w_out"], l, x2d, p["post_norm"][l])
    return x2d.reshape(bsz, seq, d)


def kernel(x_prompt, x_sample, mem_prompt, mem_sample, pre_norm, post_norm, mem_norm, w_in, w_mem_kv,
           w_branch_a, w_branch_b, w_branch_m, w_out, na_rpb, attn_sink, t5_bias):
    bf16 = jnp.bfloat16
    seq = x_prompt.shape[1]
    assert x_sample.shape[1] == seq and seq % (NA_QROWS * GRID_W) == 0 and seq % SW_BLOCK == 0
    assert seq // GRID_W >= NA_KROWS and seq % MEM_QBLOCK == 0 and seq // SW_BLOCK >= 2
    assert LANES % GRID_W == 0 and NA_KROWS % (LANES // GRID_W) == 0
    sink_col = jnp.repeat(attn_sink.astype(jnp.float32) * LOG2E, SW_BLOCK, axis=1)
    sink_col = jnp.broadcast_to(sink_col[..., None], sink_col.shape + (HEAD_DIM,))
    p = {
        "pre_norm": pre_norm, "post_norm": post_norm, "mem_norm": mem_norm,
        "w_in": w_in, "w_in_bf16": [_cast_layer(w_in, 0)] + [None] * (DEPTH - 1),
        "w_mem_kv": w_mem_kv, "w_branch_a": w_branch_a, "w_branch_b": w_branch_b,
        "w_branch_m": w_branch_m, "w_out": w_out,
        "in_scale": jnp.asarray(_in_proj_scale()),
        "kv_scale": jnp.ones((1, 2 * MEM_WIDTH), jnp.float32),
        "na_bias": _na_bias_tables(na_rpb, seq),
        "sw_bias": _sw_bias_table(t5_bias),
        "sink_col": sink_col,
    }
    y_prompt = _trunk(x_prompt, mem_prompt, p)
    y_sample = _trunk(x_sample, mem_sample, p)
    return (y_prompt, y_sample)
```

```python
import functools
import math

import numpy as np
import jax
import jax.numpy as jnp
from jax import lax
from jax.experimental import pallas as pl
from jax.experimental.pallas import tpu as pltpu

D_MODEL = 4096
DEPTH = 2
GRID_W = 64
HEAD_DIM = 128
NA_HEADS = 12
NA_ROWS = 8
NA_COLS = 16
SW_HEADS = 12
SW_KV_HEADS = 4
SW_GROUP = SW_HEADS // SW_KV_HEADS
SW_WINDOW = 128
SW_BLOCK = 128
T5_BUCKETS = 32
T5_MAX_DIST = 128
MEM_HEADS = 4
MEM_HEAD_DIM = 256
NA_WIDTH = NA_HEADS * HEAD_DIM
SW_WIDTH = SW_HEADS * HEAD_DIM
SW_KV_WIDTH = SW_KV_HEADS * HEAD_DIM
MEM_WIDTH = MEM_HEADS * MEM_HEAD_DIM
IN_SPLITS = (NA_WIDTH, NA_WIDTH, NA_WIDTH, NA_WIDTH,
             SW_WIDTH, SW_KV_WIDTH, SW_KV_WIDTH, SW_WIDTH,
             MEM_WIDTH, MEM_WIDTH,
             D_MODEL, D_MODEL, D_MODEL)
IN_WIDTH = sum(IN_SPLITS)
(OFF_QA, OFF_KA, OFF_VA, OFF_ZA, OFF_QB, OFF_KB, OFF_VB, OFF_ZB,
 OFF_QM, OFF_ZM, OFF_GA, OFF_GB, OFF_GM) = [int(c) for c in np.cumsum((0,) + IN_SPLITS[:-1])]
RMS_EPS = 1e-6
NEG_INF = -1e30
LOG2E = float(np.log2(np.e))

LANES = 128
BF16_SUBLANES = 16
VMEM_LIMIT_BYTES = 60 * 1024 * 1024
NORM_ROWS = 512
CAST_ROWS = 512
CAST_COLS = 2048
NA_HEADS_PER_STEP = 4
SW_KV_PER_STEP = 2
MEM_HEADS_PER_STEP = 2
MM_BM = 1024
MM_BN = 1024
MERGE_BM = 1024
MERGE_BN = 1024
OUT_BM = 1024
OUT_BN = 512
NA_QROWS = 4
NA_KROWS = 12
MEM_QBLOCK = 512

_NT_DIMS = (((1,), (1,)), ((), ()))


def _params(*semantics):
    return pltpu.CompilerParams(dimension_semantics=semantics, vmem_limit_bytes=VMEM_LIMIT_BYTES)


def _weighted_sum_and_total(e, v):
    hd = v.shape[1]
    r = jnp.dot(e, jnp.concatenate([v, jnp.ones_like(v)], axis=1), preferred_element_type=jnp.float32)
    return r[:, :hd], r[:, hd:]


def _sigmoid(x):
    return 0.5 * jnp.tanh(0.5 * x) + 0.5


def _rmsnorm_kernel(x_ref, g_ref, o_ref):
    x = x_ref[...]
    ms = jnp.mean(x * x, axis=-1, keepdims=True)
    o_ref[...] = ((x * lax.rsqrt(ms + RMS_EPS)) * g_ref[...]).astype(o_ref.dtype)


def _rmsnorm(x2d, gain):
    rows, d = x2d.shape
    return pl.pallas_call(
        _rmsnorm_kernel,
        grid=(rows // NORM_ROWS,),
        in_specs=[pl.BlockSpec((NORM_ROWS, d), lambda i: (i, 0)),
                  pl.BlockSpec((1, d), lambda i: (0, 0))],
        out_specs=pl.BlockSpec((NORM_ROWS, d), lambda i: (i, 0)),
        out_shape=jax.ShapeDtypeStruct((rows, d), jnp.bfloat16),
        compiler_params=_params("parallel"),
        name="rmsnorm",
    )(x2d, gain.reshape(1, d))


def _matmul_kernel(*refs, has_row_ssq, n_casts):
    x_ref, w_ref, s_ref = refs[:3]
    n_in = 3 + has_row_ssq + n_casts
    o_ref = refs[n_in]
    acc = jnp.dot(x_ref[...], w_ref[...], preferred_element_type=jnp.float32)
    if has_row_ssq:
        acc = acc * lax.rsqrt(refs[3][...] * (1.0 / x_ref.shape[1]) + RMS_EPS)
    o_ref[...] = (acc * s_ref[...]).astype(o_ref.dtype)
    for c in range(n_casts):
        dst = refs[n_in + 1 + c]
        dst[...] = refs[3 + has_row_ssq + c][...].astype(dst.dtype)


def _matmul(x2d, w_all, layer, col_scale, row_ssq=None, cast_src=None, cast_layer=None, cast_all=()):
    m, k = x2d.shape
    n = w_all.shape[2]
    bm = min(MM_BM, m)
    bn = min(MM_BN, n)
    assert m % bm == 0 and n % bn == 0
    grid = (m // bm, n // bn)
    operands = [x2d, w_all, col_scale]
    in_specs = [pl.BlockSpec((bm, k), lambda i, j: (i, 0)),
                pl.BlockSpec((None, k, bn), lambda i, j: (layer, 0, j)),
                pl.BlockSpec((1, bn), lambda i, j: (0, j))]
    out_specs = [pl.BlockSpec((bm, bn), lambda i, j: (i, j))]
    out_shape = [jax.ShapeDtypeStruct((m, n), jnp.bfloat16)]
    if row_ssq is not None:
        operands.append(row_ssq)
        in_specs.append(pl.BlockSpec((bm, 1), lambda i, j: (i, 0)))
    if cast_src is not None:
        _, k2, n2 = cast_src.shape
        assert k2 % grid[0] == 0 and n2 % grid[1] == 0
        cb = (k2 // grid[0], n2 // grid[1])
        operands.append(cast_src)
        in_specs.append(pl.BlockSpec((None,) + cb, lambda i, j: (cast_layer, i, j)))
        out_specs.append(pl.BlockSpec((None,) + cb, lambda i, j: (0, i, j)))
        out_shape.append(jax.ShapeDtypeStruct((1, k2, n2), jnp.bfloat16))
    for src in cast_all:
        depth, r, c = src.shape
        col_split = math.gcd(grid[1], c // LANES)
        lay_split = grid[1] // col_split
        assert r % (grid[0] * BF16_SUBLANES) == 0 and lay_split >= depth
        blk = (None, r // grid[0], c // col_split)

        def block_index(i, j, lay_split=lay_split, depth=depth):
            return (jnp.minimum(j % lay_split, depth - 1), i, j // lay_split)

        operands.append(src)
        in_specs.append(pl.BlockSpec(blk, block_index))
        out_specs.append(pl.BlockSpec(blk, block_index))
        out_shape.append(jax.ShapeDtypeStruct(src.shape, jnp.bfloat16))
    n_casts = (cast_src is not None) + len(cast_all)
    outs = pl.pallas_call(
        functools.partial(_matmul_kernel, has_row_ssq=row_ssq is not None, n_casts=n_casts),
        grid=grid, in_specs=in_specs, out_specs=out_specs, out_shape=out_shape,
        compiler_params=_params("parallel", "arbitrary"), name="proj_matmul",
    )(*operands)
    return outs[0] if n_casts == 0 else tuple(outs)


def _cast_kernel(src_ref, dst_ref):
    dst_ref[...] = src_ref[...].astype(dst_ref.dtype)


def _cast_layer(w_all, layer):
    _, k, n = w_all.shape
    bk, bn = min(CAST_ROWS, k), min(CAST_COLS, n)
    assert k % bk == 0 and n % bn == 0
    return pl.pallas_call(
        _cast_kernel,
        grid=(k // bk, n // bn),
        in_specs=[pl.BlockSpec((None, bk, bn), lambda i, j: (layer, i, j))],
        out_specs=pl.BlockSpec((None, bk, bn), lambda i, j: (0, i, j)),
        out_shape=jax.ShapeDtypeStruct((1, k, n), jnp.bfloat16),
        compiler_params=_params("parallel", "parallel"),
        name="cast_weights",
    )(w_all)


def _na_kernel(q_ref, k_ref, v_ref, z_ref, bias_ref, o_ref):
    seq = q_ref.shape[0]
    qb = NA_QROWS * GRID_W
    kb = NA_KROWS * GRID_W
    nblk = seq // qb
    grid_rows = seq // GRID_W

    for hh in range(NA_HEADS_PER_STEP):
        lanes = slice(hh * HEAD_DIM, (hh + 1) * HEAD_DIM)
        for i in range(nblk):
            case = 0 if i == 0 else (2 if i == nblk - 1 else 1)
            q0 = i * qb
            k0 = min(max(NA_QROWS * i - NA_ROWS // 2, 0), grid_rows - NA_KROWS) * GRID_W
            q = q_ref[q0:q0 + qb, lanes]
            k = k_ref[k0:k0 + kb, lanes]
            v = v_ref[k0:k0 + kb, lanes]
            s = lax.dot_general(q, k, _NT_DIMS, preferred_element_type=jnp.float32) + bias_ref[hh, case]
            mx = jnp.max(s, axis=-1, keepdims=True)
            num, tot = _weighted_sum_and_total(jnp.exp2(s - mx).astype(v.dtype), v)
            o = num / tot
            z = z_ref[q0:q0 + qb, lanes].astype(jnp.float32)
            o_ref[q0:q0 + qb, lanes] = (o * (z * _sigmoid(z))).astype(o_ref.dtype)


def _na_attention(proj, bias_all, layer):
    bsz, seq, _ = proj.shape
    qb = NA_QROWS * GRID_W
    kb = NA_KROWS * GRID_W

    hps = NA_HEADS_PER_STEP
    width = hps * HEAD_DIM
    assert NA_HEADS % hps == 0 and all(off % width == 0 for off in (OFF_QA, OFF_KA, OFF_VA, OFF_ZA))

    def col(off):
        return pl.BlockSpec((None, seq, width), lambda h, b: (b, 0, off // width + h))

    return pl.pallas_call(
        _na_kernel,
        grid=(NA_HEADS // hps, bsz),
        in_specs=[col(OFF_QA), col(OFF_KA), col(OFF_VA), col(OFF_ZA),
                  pl.BlockSpec((None, hps, 3, qb, kb), lambda h, b: (layer, h, 0, 0, 0))],
        out_specs=pl.BlockSpec((None, seq, width), lambda h, b: (b, 0, h)),
        out_shape=jax.ShapeDtypeStruct((bsz, seq, NA_WIDTH), jnp.bfloat16),
        compiler_params=_params("parallel", "arbitrary"),
        name="na_attention",
    )(proj, proj, proj, proj, bias_all)


def _na_bias_tables(rpb, seq):
    grid_rows = seq // GRID_W
    nblk = grid_rows // NA_QROWS
    win_rows = min(NA_ROWS, grid_rows)
    qc = np.arange(GRID_W)
    cstart = np.clip(qc - NA_COLS // 2, 0, GRID_W - NA_COLS)
    kc = np.arange(GRID_W)
    col_ok = (kc[None, :] >= cstart[:, None]) & (kc[None, :] < cstart[:, None] + NA_COLS)
    dc = np.clip(kc[None, :] - qc[:, None], -(NA_COLS - 1), NA_COLS - 1) + NA_COLS - 1
    row_ok, dr = [], []
    for i in (0, 1, nblk - 1):
        qr = NA_QROWS * i + np.arange(NA_QROWS)
        kr = np.clip(NA_QROWS * i - NA_ROWS // 2, 0, grid_rows - NA_KROWS) + np.arange(NA_KROWS)
        rs = np.clip(qr - win_rows // 2, 0, grid_rows - win_rows)
        ok = (kr[None, :] >= rs[:, None]) & (kr[None, :] < rs[:, None] + win_rows)
        row_ok.append(ok)
        dr.append(np.clip(kr[None, :] - qr[:, None] + NA_ROWS - 1, 0, 2 * NA_ROWS - 2))
    row_ok = np.stack(row_ok)
    dr = np.stack(dr)
    nd, ne = 2 * NA_ROWS - 1, 2 * NA_COLS - 1
    row_hot = (dr[..., None] == np.arange(nd)).astype(np.float32)
    col_hot = (dc[..., None] == np.arange(ne)).astype(np.float32)
    pair = LANES // GRID_W
    pair_hot = np.zeros((pair, ne, GRID_W, pair, GRID_W), np.float32)
    for b in range(pair):
        pair_hot[b, :, :, b, :] = np.transpose(col_hot, (2, 0, 1))
    pair_hot = pair_hot.reshape(pair * ne, GRID_W, LANES)
    row_sel = np.zeros((pair, GRID_W, pair, GRID_W), np.float32)
    for b in range(pair):
        row_sel[b, :, b, :] = 1.0
    col_bad = np.broadcast_to((~col_ok)[:, None, :], (GRID_W, pair, GRID_W)).astype(np.float32)
    feats = np.concatenate([pair_hot, row_sel.reshape(pair, GRID_W, LANES),
                            col_bad.reshape(1, GRID_W, LANES)], axis=0)
    row_bad = (~row_ok).astype(np.float32).reshape(3, NA_QROWS, NA_KROWS // pair, pair) * NEG_INF
    nl, nh = rpb.shape[0], rpb.shape[1]
    g = jnp.einsum("sabd,lhde->lhsabe", jnp.asarray(row_hot), rpb.astype(jnp.float32) * LOG2E,
                   precision=lax.Precision.HIGHEST)
    g = g.reshape(nl, nh, 3, NA_QROWS, NA_KROWS // pair, pair * ne)
    lead = g.shape[:-1]
    g = jnp.concatenate([g, jnp.broadcast_to(jnp.asarray(row_bad), lead + (pair,)),
                         jnp.full(lead + (1,), NEG_INF, jnp.float32)], axis=-1)
    t = jnp.einsum("lhsapk,kxz->lhsaxpz", g, jnp.asarray(feats), precision=lax.Precision.HIGHEST)
    return t.reshape(nl, nh, 3, NA_QROWS * GRID_W, NA_KROWS * GRID_W)


def _sw_kernel(q_ref, k_ref, v_ref, *rest):
    *z_refs, bias_ref, sink_ref, o_ref = rest
    seq = k_ref.shape[0]
    blk = SW_BLOCK
    nblk = seq // blk
    gw = SW_GROUP * HEAD_DIM
    grows = SW_GROUP * blk

    for kv in range(SW_KV_PER_STEP):
        sk = sink_ref[kv * grows:(kv + 1) * grows, :]
        for i in range(nblk):
            r0 = i * blk
            lo = max(i - 1, 0) * blk
            hi = min(i + 2, nblk) * blk
            c0 = lo - (i - 1) * blk
            qs = jnp.concatenate(
                [q_ref[r0:r0 + blk, kv * gw + g * HEAD_DIM:kv * gw + (g + 1) * HEAD_DIM]
                 for g in range(SW_GROUP)], axis=0)
            kw = k_ref[lo:hi, kv * HEAD_DIM:(kv + 1) * HEAD_DIM]
            vw = v_ref[lo:hi, kv * HEAD_DIM:(kv + 1) * HEAD_DIM]
            s = (lax.dot_general(qs, kw, _NT_DIMS, preferred_element_type=jnp.float32)
                 + bias_ref[kv * grows:(kv + 1) * grows, c0:c0 + hi - lo])
            mx = jnp.maximum(jnp.max(s, axis=-1, keepdims=True), sk)
            mx_wide = jnp.concatenate([mx] * ((hi - lo) // HEAD_DIM), axis=1)
            num, tot = _weighted_sum_and_total(jnp.exp2(s - mx_wide).astype(vw.dtype), vw)
            o = num / (tot + jnp.exp2(sk - mx))
            og = jnp.concatenate([o[g * blk:(g + 1) * blk, :] for g in range(SW_GROUP)], axis=1)
            z = jnp.concatenate([r[r0:r0 + blk, :] for r in z_refs], axis=1)[:, kv * gw:(kv + 1) * gw]
            z = z.astype(jnp.float32)
            o_ref[r0:r0 + blk, kv * gw:(kv + 1) * gw] = (og * (z * _sigmoid(z))).astype(o_ref.dtype)


def _sw_attention(proj, bias, sink_col):
    bsz, seq, _ = proj.shape
    kvs = SW_KV_PER_STEP
    qw = kvs * SW_GROUP * HEAD_DIM
    kw = kvs * HEAD_DIM
    zw = math.gcd(OFF_ZB, qw)
    nz = qw // zw
    rows = kvs * SW_GROUP * SW_BLOCK
    assert SW_KV_HEADS % kvs == 0 and OFF_QB % qw == 0 and OFF_KB % kw == 0 and OFF_VB % kw == 0
    assert zw % LANES == 0

    def cols(off, width):
        return pl.BlockSpec((None, seq, width), lambda k, b: (b, 0, off // width + k))

    def gate(t):
        return pl.BlockSpec((None, seq, zw), lambda k, b: (b, 0, OFF_ZB // zw + k * nz + t))

    return pl.pallas_call(
        _sw_kernel,
        grid=(SW_KV_HEADS // kvs, bsz),
        in_specs=[cols(OFF_QB, qw), cols(OFF_KB, kw), cols(OFF_VB, kw)] + [gate(t) for t in range(nz)]
        + [pl.BlockSpec((rows, 3 * SW_BLOCK), lambda k, b: (k, 0)),
           pl.BlockSpec((rows, HEAD_DIM), lambda k, b: (k, 0))],
        out_specs=pl.BlockSpec((None, seq, qw), lambda k, b: (b, 0, k)),
        out_shape=jax.ShapeDtypeStruct((bsz, seq, SW_WIDTH), jnp.bfloat16),
        compiler_params=_params("parallel", "arbitrary"),
        name="sw_attention",
    )(*([proj] * (3 + nz)), bias, sink_col)


def _t5_bucket_index(rel):
    nb = T5_BUCKETS // 2
    max_exact = nb // 2
    ret = (rel > 0).astype(np.int32) * nb
    n = np.abs(rel)
    large = max_exact + (np.log(np.maximum(n, 1) / max_exact)
                         / np.log(T5_MAX_DIST / max_exact) * (nb - max_exact)).astype(np.int32)
    large = np.minimum(large, nb - 1)
    return (ret + np.where(n < max_exact, n, large)).astype(np.int32)


def _sw_bias_table(t5_bias):
    rel = np.arange(3 * SW_BLOCK)[None, :] - SW_BLOCK - np.arange(SW_BLOCK)[:, None]
    band = np.abs(rel) <= SW_WINDOW
    hot = (_t5_bucket_index(rel)[..., None] == np.arange(T5_BUCKETS)).astype(np.float32)
    t = jnp.einsum("qkb,bh->hqk", jnp.asarray(hot), t5_bias.astype(jnp.float32),
                   precision=lax.Precision.HIGHEST)
    t = jnp.where(jnp.asarray(band)[None], t * LOG2E, NEG_INF)
    return t.reshape(SW_HEADS * SW_BLOCK, 3 * SW_BLOCK)


def _mem_kernel(q_ref, z_ref, k_ref, v_ref, o_ref):
    seq = q_ref.shape[0]
    for hh in range(MEM_HEADS_PER_STEP):
        lanes = slice(hh * MEM_HEAD_DIM, (hh + 1) * MEM_HEAD_DIM)
        k = k_ref[:, lanes]
        v = v_ref[:, lanes]
        for i in range(seq // MEM_QBLOCK):
            r0 = i * MEM_QBLOCK
            q = q_ref[r0:r0 + MEM_QBLOCK, lanes]
            s = lax.dot_general(q, k, _NT_DIMS, preferred_element_type=jnp.float32)
            mx = jnp.max(s, axis=-1, keepdims=True)
            e = jnp.exp2(s - mx)
            den = jnp.sum(e, axis=-1, keepdims=True)
            o = jnp.dot(e.astype(v.dtype), v, preferred_element_type=jnp.float32) / den
            z = z_ref[r0:r0 + MEM_QBLOCK, lanes].astype(jnp.float32)
            o_ref[r0:r0 + MEM_QBLOCK, lanes] = (o * (z * _sigmoid(z))).astype(o_ref.dtype)


def _mem_attention(proj, memkv):
    bsz, seq, _ = proj.shape
    mtok = memkv.shape[1]

    width = MEM_HEADS_PER_STEP * MEM_HEAD_DIM

    def qcol(off):
        return pl.BlockSpec((None, seq, width), lambda h, b: (b, 0, off // width + h))

    def kvcol(off):
        return pl.BlockSpec((None, mtok, width), lambda h, b: (b, 0, off // width + h))

    assert MEM_HEADS % MEM_HEADS_PER_STEP == 0
    assert OFF_QM % width == 0 and OFF_ZM % width == 0 and MEM_WIDTH % width == 0
    return pl.pallas_call(
        _mem_kernel,
        grid=(MEM_HEADS // MEM_HEADS_PER_STEP, bsz),
        in_specs=[qcol(OFF_QM), qcol(OFF_ZM), kvcol(0), kvcol(MEM_WIDTH)],
        out_specs=pl.BlockSpec((None, seq, width), lambda h, b: (b, 0, h)),
        out_shape=jax.ShapeDtypeStruct((bsz, seq, MEM_WIDTH), jnp.bfloat16),
        compiler_params=_params("parallel", "arbitrary"),
        name="mem_attention",
    )(proj, proj, memkv, memkv)


def _merge_kernel(a_ref, b_ref, m_ref, wa_ref, wb_ref, wm_ref, ga_ref, gb_ref, gm_ref, o_ref):
    def branch(x_ref, w_ref, g_ref):
        y = jnp.dot(x_ref[...], w_ref[...], preferred_element_type=jnp.float32)
        return _sigmoid(g_ref[...].astype(jnp.float32)) * y

    acc = branch(a_ref, wa_ref, ga_ref)
    acc = acc + branch(b_ref, wb_ref, gb_ref)
    acc = acc + branch(m_ref, wm_ref, gm_ref)
    o_ref[...] = acc.astype(o_ref.dtype)


def _merge(a2d, b2d, m2d, wa, wb, wm, layer, proj2d):
    rows = a2d.shape[0]
    bm = min(MERGE_BM, rows)
    bn = MERGE_BN

    def act(width):
        return pl.BlockSpec((bm, width), lambda i, j: (i, 0))

    def wgt(width):
        return pl.BlockSpec((None, width, bn), lambda i, j: (layer, 0, j))

    def gate(off):
        return pl.BlockSpec((bm, bn), lambda i, j: (i, off // bn + j))

    assert OFF_GA % bn == 0 and OFF_GB % bn == 0 and OFF_GM % bn == 0 and rows % bm == 0
    return pl.pallas_call(
        _merge_kernel,
        grid=(rows // bm, D_MODEL // bn),
        in_specs=[act(NA_WIDTH), act(SW_WIDTH), act(MEM_WIDTH),
                  wgt(NA_WIDTH), wgt(SW_WIDTH), wgt(MEM_WIDTH),
                  gate(OFF_GA), gate(OFF_GB), gate(OFF_GM)],
        out_specs=pl.BlockSpec((bm, bn), lambda i, j: (i, j)),
        out_shape=jax.ShapeDtypeStruct((rows, D_MODEL), jnp.bfloat16),
        compiler_params=_params("parallel", "arbitrary"),
        name="branch_merge",
    )(a2d, b2d, m2d, wa, wb, wm, proj2d, proj2d, proj2d)


def _outproj_kernel(*refs, emit_next):
    if emit_next:
        mg_ref, w_ref, x_ref, g_ref, gn_ref, o_ref, xg_ref, ssqn_ref, y_ref, ssq_ref, ssqn_acc = refs
    else:
        mg_ref, w_ref, x_ref, g_ref, o_ref, y_ref, ssq_ref = refs
    i = pl.program_id(0)
    j = pl.program_id(1)
    n_tiles = pl.num_programs(0) - 1
    bn = w_ref.shape[1]
    d = y_ref.shape[1]
    col = pl.multiple_of(j * bn, bn)
    cur = lax.rem(i, 2)

    def finish():
        inv = lax.rsqrt(ssq_ref[1 - cur] * (1.0 / d) + RMS_EPS)
        x_new = x_ref[...] + (y_ref[:, pl.ds(col, bn)] * inv) * g_ref[...]
        o_ref[...] = x_new
        if emit_next:
            xg_ref[...] = (x_new * gn_ref[...]).astype(xg_ref.dtype)
            total = jnp.sum(x_new * x_new, axis=-1, keepdims=True) + jnp.where(j == 0, 0.0, ssqn_acc[...])
            ssqn_acc[...] = total
            ssqn_ref[...] = total

    def project():
        y = jnp.dot(mg_ref[...], w_ref[...], preferred_element_type=jnp.float32)
        y_ref[:, pl.ds(col, bn)] = y
        part = jnp.sum(y * y, axis=-1, keepdims=True)
        ssq_ref[cur] = part + jnp.where(j == 0, 0.0, ssq_ref[cur])

    @pl.when(i == 0)
    def _():
        @pl.when(j == 0)
        def _():
            ssq_ref[...] = jnp.zeros_like(ssq_ref)
            if emit_next:
                ssqn_acc[...] = jnp.zeros_like(ssqn_acc)
        project()

    @pl.when((i > 0) & (i < n_tiles))
    def _():
        finish()
        project()

    @pl.when(i == n_tiles)
    def _():
        finish()


def _outproj(merged2d, w_all, layer, x2d, gain, next_gain=None):
    rows, d = x2d.shape
    bm = min(OUT_BM, rows)
    bn = OUT_BN
    n_tiles = rows // bm

    def finished_block(i, j):
        return (jnp.maximum(i - 1, 0), jnp.where(i == 0, 0, j))

    emit_next = next_gain is not None
    operands = [merged2d, w_all, x2d, gain.reshape(1, d)]
    in_specs = [pl.BlockSpec((bm, d), lambda i, j: (jnp.minimum(i, n_tiles - 1), 0)),
                pl.BlockSpec((None, d, bn), lambda i, j: (layer, 0, j)),
                pl.BlockSpec((bm, bn), finished_block),
                pl.BlockSpec((1, bn), lambda i, j: (0, j))]
    out_specs = [pl.BlockSpec((bm, bn), finished_block)]
    out_shape = [jax.ShapeDtypeStruct((rows, d), jnp.float32)]
    if emit_next:
        operands.append(next_gain.reshape(1, d))
        in_specs.append(pl.BlockSpec((1, bn), lambda i, j: (0, j)))
        out_specs += [pl.BlockSpec((bm, bn), finished_block),
                      pl.BlockSpec((bm, 1), lambda i, j: (jnp.maximum(i - 1, 0), 0))]
        out_shape += [jax.ShapeDtypeStruct((rows, d), jnp.bfloat16),
                      jax.ShapeDtypeStruct((rows, 1), jnp.float32)]
    outs = pl.pallas_call(
        functools.partial(_outproj_kernel, emit_next=emit_next),
        grid=(n_tiles + 1, d // bn),
        in_specs=in_specs, out_specs=out_specs, out_shape=out_shape,
        scratch_shapes=[pltpu.VMEM((bm, d), jnp.float32),
                        pltpu.VMEM((2, bm, 1), jnp.float32)]
        + ([pltpu.VMEM((bm, 1), jnp.float32)] if emit_next else []),
        compiler_params=_params("arbitrary", "arbitrary"),
        name="out_proj_norm",
    )(*operands)
    return tuple(outs) if emit_next else outs[0]


_SIDE_CAST_WEIGHTS = ("w_mem_kv", "w_branch_a", "w_branch_b", "w_branch_m", "w_out")


def _in_proj_scale():
    s = np.ones((1, IN_WIDTH), np.float32)
    s[0, OFF_QA:OFF_QA + NA_WIDTH] = HEAD_DIM ** -0.5 * LOG2E
    s[0, OFF_QB:OFF_QB + SW_WIDTH] = HEAD_DIM ** -0.5 * LOG2E
    s[0, OFF_QM:OFF_QM + MEM_WIDTH] = MEM_HEAD_DIM ** -0.5 * LOG2E
    return s


def _trunk(x, mem, p):
    bsz, seq, d = x.shape
    mtok = mem.shape[1]
    x2d = x.reshape(bsz * seq, d)
    mem2d = mem.reshape(bsz * mtok, d)
    h, h_ssq = _rmsnorm(x2d, p["pre_norm"][0]), None
    for l in range(DEPTH):
        w_in_bf = p["w_in_bf16"]
        pending = [name for name in _SIDE_CAST_WEIGHTS if p[name].dtype != jnp.bfloat16]
        next_w_in = l + 1 < DEPTH and w_in_bf[l + 1] is None
        outs = _matmul(h, w_in_bf[l], 0, p["in_scale"], row_ssq=h_ssq,
                       cast_src=p["w_in"] if next_w_in else None, cast_layer=l + 1,
                       cast_all=[p[name] for name in pending])
        if next_w_in or pending:
            proj2d, copies = outs[0], list(outs[1:])
            if next_w_in:
                w_in_bf[l + 1] = copies.pop(0)
            p.update(zip(pending, copies))
        else:
            proj2d = outs
        proj = proj2d.reshape(bsz, seq, IN_WIDTH)
        mem_h = _rmsnorm(mem2d, p["mem_norm"][l])
        memkv = _matmul(mem_h, p["w_mem_kv"], l, p["kv_scale"]).reshape(bsz, mtok, 2 * MEM_WIDTH)
        a = _na_attention(proj, p["na_bias"], l)
        b = _sw_attention(proj, p["sw_bias"], p["sink_col"][l])
        m = _mem_attention(proj, memkv)
        merged = _merge(a.reshape(bsz * seq, NA_WIDTH), b.reshape(bsz * seq, SW_WIDTH),
                        m.reshape(bsz * seq, MEM_WIDTH), p["w_branch_a"], p["w_branch_b"],
                        p["w_branch_m"], l, proj2d)
        if l + 1 < DEPTH:
            x2d, h, h_ssq = _outproj(merged, p["w_out"], l, x2d, p["post_norm"][l],
                                     next_gain=p["pre_norm"][l + 1])
        else:
            x2d = _outproj(merged, p["w_out"], l, x2d, p["post_norm"][l])
    return x2d.reshape(bsz, seq, d)


def kernel(x_prompt, x_sample, mem_prompt, mem_sample, pre_norm, post_norm, mem_norm, w_in, w_mem_kv,
           w_branch_a, w_branch_b, w_branch_m, w_out, na_rpb, attn_sink, t5_bias):
    bf16 = jnp.bfloat16
    seq = x_prompt.shape[1]
    assert x_sample.shape[1] == seq and seq % (NA_QROWS * GRID_W) == 0 and seq % SW_BLOCK == 0
    assert seq // GRID_W >= NA_KROWS and seq % MEM_QBLOCK == 0 and seq // SW_BLOCK >= 2
    assert LANES % GRID_W == 0 and NA_KROWS % (LANES // GRID_W) == 0
    sink_col = jnp.repeat(attn_sink.astype(jnp.float32) * LOG2E, SW_BLOCK, axis=1)
    sink_col = jnp.broadcast_to(sink_col[..., None], sink_col.shape + (HEAD_DIM,))
    p = {
        "pre_norm": pre_norm, "post_norm": post_norm, "mem_norm": mem_norm,
        "w_in": w_in, "w_in_bf16": [_cast_layer(w_in, 0)] + [None] * (DEPTH - 1),
        "w_mem_kv": w_mem_kv, "w_branch_a": w_branch_a, "w_branch_b": w_branch_b,
        "w_branch_m": w_branch_m, "w_out": w_out,
        "in_scale": jnp.asarray(_in_proj_scale()),
        "kv_scale": jnp.ones((1, 2 * MEM_WIDTH), jnp.float32),
        "na_bias": _na_bias_tables(na_rpb, seq),
        "sw_bias": _sw_bias_table(t5_bias),
        "sink_col": sink_col,
    }
    y_prompt = _trunk(x_prompt, mem_prompt, p)
    y_sample = _trunk(x_sample, mem_sample, p)
    return (y_prompt, y_sample)
```

```python
import functools
import math

import numpy as np
import jax
import jax.numpy as jnp
from jax import lax
from jax.experimental import pallas as pl
from jax.experimental.pallas import tpu as pltpu

D_MODEL = 4096
DEPTH = 2
GRID_W = 64
HEAD_DIM = 128
NA_HEADS = 12
NA_ROWS = 8
NA_COLS = 16
SW_HEADS = 12
SW_KV_HEADS = 4
SW_GROUP = SW_HEADS // SW_KV_HEADS
SW_WINDOW = 128
SW_BLOCK = 128
T5_BUCKETS = 32
T5_MAX_DIST = 128
MEM_HEADS = 4
MEM_HEAD_DIM = 256
NA_WIDTH = NA_HEADS * HEAD_DIM
SW_WIDTH = SW_HEADS * HEAD_DIM
SW_KV_WIDTH = SW_KV_HEADS * HEAD_DIM
MEM_WIDTH = MEM_HEADS * MEM_HEAD_DIM
IN_SPLITS = (NA_WIDTH, NA_WIDTH, NA_WIDTH, NA_WIDTH,
             SW_WIDTH, SW_KV_WIDTH, SW_KV_WIDTH, SW_WIDTH,
             MEM_WIDTH, MEM_WIDTH,
             D_MODEL, D_MODEL, D_MODEL)
IN_WIDTH = sum(IN_SPLITS)
(OFF_QA, OFF_KA, OFF_VA, OFF_ZA, OFF_QB, OFF_KB, OFF_VB, OFF_ZB,
 OFF_QM, OFF_ZM, OFF_GA, OFF_GB, OFF_GM) = [int(c) for c in np.cumsum((0,) + IN_SPLITS[:-1])]
RMS_EPS = 1e-6
NEG_INF = -1e30
LOG2E = float(np.log2(np.e))

LANES = 128
BF16_SUBLANES = 16
VMEM_LIMIT_BYTES = 60 * 1024 * 1024
NORM_ROWS = 512
CAST_ROWS = 512
CAST_COLS = 2048
NA_HEADS_PER_STEP = 4
SW_KV_PER_STEP = 2
MEM_HEADS_PER_STEP = 2
MM_BM = 1024
MM_BN = 1024
MERGE_BM = 1024
MERGE_BN = 1024
OUT_BM = 1024
OUT_BN = 512
NA_QROWS = 4
NA_KROWS = 12
MEM_QBLOCK = 512

_NT_DIMS = (((1,), (1,)), ((), ()))


def _params(*semantics):
    return pltpu.CompilerParams(dimension_semantics=semantics, vmem_limit_bytes=VMEM_LIMIT_BYTES)


def _weighted_sum_and_total(e, v):
    hd = v.shape[1]
    r = jnp.dot(e, jnp.concatenate([v, jnp.ones_like(v)], axis=1), preferred_element_type=jnp.float32)
    return r[:, :hd], r[:, hd:]


def _sigmoid(x):
    return 0.5 * jnp.tanh(0.5 * x) + 0.5


def _rmsnorm_kernel(x_ref, g_ref, o_ref):
    x = x_ref[...]
    ms = jnp.mean(x * x, axis=-1, keepdims=True)
    o_ref[...] = ((x * lax.rsqrt(ms + RMS_EPS)) * g_ref[...]).astype(o_ref.dtype)


def _rmsnorm(x2d, gain):
    rows, d = x2d.shape
    return pl.pallas_call(
        _rmsnorm_kernel,
        grid=(rows // NORM_ROWS,),
        in_specs=[pl.BlockSpec((NORM_ROWS, d), lambda i: (i, 0)),
                  pl.BlockSpec((1, d), lambda i: (0, 0))],
        out_specs=pl.BlockSpec((NORM_ROWS, d), lambda i: (i, 0)),
        out_shape=jax.ShapeDtypeStruct((rows, d), jnp.bfloat16),
        compiler_params=_params("parallel"),
        name="rmsnorm",
    )(x2d, gain.reshape(1, d))


def _matmul_kernel(*refs, has_row_ssq, n_casts):
    x_ref, w_ref, s_ref = refs[:3]
    n_in = 3 + has_row_ssq + n_casts
    o_ref = refs[n_in]
    acc = jnp.dot(x_ref[...], w_ref[...], preferred_element_type=jnp.float32)
    if has_row_ssq:
        acc = acc * lax.rsqrt(refs[3][...] * (1.0 / x_ref.shape[1]) + RMS_EPS)
    o_ref[...] = (acc * s_ref[...]).astype(o_ref.dtype)
    for c in range(n_casts):
        dst = refs[n_in + 1 + c]
        dst[...] = refs[3 + has_row_ssq + c][...].astype(dst.dtype)


def _matmul(x2d, w_all, layer, col_scale, row_ssq=None, cast_src=None, cast_layer=None, cast_all=()):
    m, k = x2d.shape
    n = w_all.shape[2]
    bm = min(MM_BM, m)
    bn = min(MM_BN, n)
    assert m % bm == 0 and n % bn == 0
    grid = (m // bm, n // bn)
    operands = [x2d, w_all, col_scale]
    in_specs = [pl.BlockSpec((bm, k), lambda i, j: (i, 0)),
                pl.BlockSpec((None, k, bn), lambda i, j: (layer, 0, j)),
                pl.BlockSpec((1, bn), lambda i, j: (0, j))]
    out_specs = [pl.BlockSpec((bm, bn), lambda i, j: (i, j))]
    out_shape = [jax.ShapeDtypeStruct((m, n), jnp.bfloat16)]
    if row_ssq is not None:
        operands.append(row_ssq)
        in_specs.append(pl.BlockSpec((bm, 1), lambda i, j: (i, 0)))
    if cast_src is not None:
        _, k2, n2 = cast_src.shape
        assert k2 % grid[0] == 0 and n2 % grid[1] == 0
        cb = (k2 // grid[0], n2 // grid[1])
        operands.append(cast_src)
        in_specs.append(pl.BlockSpec((None,) + cb, lambda i, j: (cast_layer, i, j)))
        out_specs.append(pl.BlockSpec((None,) + cb, lambda i, j: (0, i, j)))
        out_shape.append(jax.ShapeDtypeStruct((1, k2, n2), jnp.bfloat16))
    for src in cast_all:
        depth, r, c = src.shape
        col_split = math.gcd(grid[1], c // LANES)
        lay_split = grid[1] // col_split
        assert r % (grid[0] * BF16_SUBLANES) == 0 and lay_split >= depth
        blk = (None, r // grid[0], c // col_split)

        def block_index(i, j, lay_split=lay_split, depth=depth):
            return (jnp.minimum(j % lay_split, depth - 1), i, j // lay_split)

        operands.append(src)
        in_specs.append(pl.BlockSpec(blk, block_index))
        out_specs.append(pl.BlockSpec(blk, block_index))
        out_shape.append(jax.ShapeDtypeStruct(src.shape, jnp.bfloat16))
    n_casts = (cast_src is not None) + len(cast_all)
    outs = pl.pallas_call(
        functools.partial(_matmul_kernel, has_row_ssq=row_ssq is not None, n_casts=n_casts),
        grid=grid, in_specs=in_specs, out_specs=out_specs, out_shape=out_shape,
        compiler_params=_params("parallel", "arbitrary"), name="proj_matmul",
    )(*operands)
    return outs[0] if n_casts == 0 else tuple(outs)


def _cast_kernel(src_ref, dst_ref):
    dst_ref[...] = src_ref[...].astype(dst_ref.dtype)


def _cast_layer(w_all, layer):
    _, k, n = w_all.shape
    bk, bn = min(CAST_ROWS, k), min(CAST_COLS, n)
    assert k % bk == 0 and n % bn == 0
    return pl.pallas_call(
        _cast_kernel,
        grid=(k // bk, n // bn),
        in_specs=[pl.BlockSpec((None, bk, bn), lambda i, j: (layer, i, j))],
        out_specs=pl.BlockSpec((None, bk, bn), lambda i, j: (0, i, j)),
        out_shape=jax.ShapeDtypeStruct((1, k, n), jnp.bfloat16),
        compiler_params=_params("parallel", "parallel"),
        name="cast_weights",
    )(w_all)


def _na_kernel(q_ref, k_ref, v_ref, z_ref, bias_ref, o_ref):
    seq = q_ref.shape[0]
    qb = NA_QROWS * GRID_W
    kb = NA_KROWS * GRID_W
    nblk = seq // qb
    grid_rows = seq // GRID_W

    for hh in range(NA_HEADS_PER_STEP):
        lanes = slice(hh * HEAD_DIM, (hh + 1) * HEAD_DIM)
        for i in range(nblk):
            case = 0 if i == 0 else (2 if i == nblk - 1 else 1)
            q0 = i * qb
            k0 = min(max(NA_QROWS * i - NA_ROWS // 2, 0), grid_rows - NA_KROWS) * GRID_W
            q = q_ref[q0:q0 + qb, lanes]
            k = k_ref[k0:k0 + kb, lanes]
            v = v_ref[k0:k0 + kb, lanes]
            bias = jnp.concatenate(
                [jnp.concatenate([bias_ref[hh, case, a, p] for p in range(bias_ref.shape[3])], axis=1)
                 for a in range(bias_ref.shape[2])], axis=0)
            s = lax.dot_general(q, k, _NT_DIMS, preferred_element_type=jnp.float32) + bias
            mx = jnp.max(s, axis=-1, keepdims=True)
            num, tot = _weighted_sum_and_total(jnp.exp2(s - mx).astype(v.dtype), v)
            o = num / tot
            z = z_ref[q0:q0 + qb, lanes].astype(jnp.float32)
            o_ref[q0:q0 + qb, lanes] = (o * (z * _sigmoid(z))).astype(o_ref.dtype)


def _na_attention(proj, bias_all, layer):
    bsz, seq, _ = proj.shape

    hps = NA_HEADS_PER_STEP
    width = hps * HEAD_DIM
    assert NA_HEADS % hps == 0 and all(off % width == 0 for off in (OFF_QA, OFF_KA, OFF_VA, OFF_ZA))

    def col(off):
        return pl.BlockSpec((None, seq, width), lambda h, b: (b, 0, off // width + h))

    return pl.pallas_call(
        _na_kernel,
        grid=(NA_HEADS // hps, bsz),
        in_specs=[col(OFF_QA), col(OFF_KA), col(OFF_VA), col(OFF_ZA),
                  pl.BlockSpec((None, hps) + bias_all.shape[2:], lambda h, b: (layer, h, 0, 0, 0, 0, 0))],
        out_specs=pl.BlockSpec((None, seq, width), lambda h, b: (b, 0, h)),
        out_shape=jax.ShapeDtypeStruct((bsz, seq, NA_WIDTH), jnp.bfloat16),
        compiler_params=_params("parallel", "arbitrary"),
        name="na_attention",
    )(proj, proj, proj, proj, bias_all)


def _na_bias_tables(rpb, seq):
    grid_rows = seq // GRID_W
    nblk = grid_rows // NA_QROWS
    win_rows = min(NA_ROWS, grid_rows)
    qc = np.arange(GRID_W)
    cstart = np.clip(qc - NA_COLS // 2, 0, GRID_W - NA_COLS)
    kc = np.arange(GRID_W)
    col_ok = (kc[None, :] >= cstart[:, None]) & (kc[None, :] < cstart[:, None] + NA_COLS)
    dc = np.clip(kc[None, :] - qc[:, None], -(NA_COLS - 1), NA_COLS - 1) + NA_COLS - 1
    row_ok, dr = [], []
    for i in (0, 1, nblk - 1):
        qr = NA_QROWS * i + np.arange(NA_QROWS)
        kr = np.clip(NA_QROWS * i - NA_ROWS // 2, 0, grid_rows - NA_KROWS) + np.arange(NA_KROWS)
        rs = np.clip(qr - win_rows // 2, 0, grid_rows - win_rows)
        ok = (kr[None, :] >= rs[:, None]) & (kr[None, :] < rs[:, None] + win_rows)
        row_ok.append(ok)
        dr.append(np.clip(kr[None, :] - qr[:, None] + NA_ROWS - 1, 0, 2 * NA_ROWS - 2))
    row_ok = np.stack(row_ok)
    dr = np.stack(dr)
    nd, ne = 2 * NA_ROWS - 1, 2 * NA_COLS - 1
    row_hot = (dr[..., None] == np.arange(nd)).astype(np.float32)
    col_hot = (dc[..., None] == np.arange(ne)).astype(np.float32)
    pair = LANES // GRID_W
    pair_hot = np.zeros((pair, ne, GRID_W, pair, GRID_W), np.float32)
    for b in range(pair):
        pair_hot[b, :, :, b, :] = np.transpose(col_hot, (2, 0, 1))
    pair_hot = pair_hot.reshape(pair * ne, GRID_W, LANES)
    row_sel = np.zeros((pair, GRID_W, pair, GRID_W), np.float32)
    for b in range(pair):
        row_sel[b, :, b, :] = 1.0
    col_bad = np.broadcast_to((~col_ok)[:, None, :], (GRID_W, pair, GRID_W)).astype(np.float32)
    feats = np.concatenate([pair_hot, row_sel.reshape(pair, GRID_W, LANES),
                            col_bad.reshape(1, GRID_W, LANES)], axis=0)
    row_bad = (~row_ok).astype(np.float32).reshape(3, NA_QROWS, NA_KROWS // pair, pair) * NEG_INF
    nl, nh = rpb.shape[0], rpb.shape[1]
    g = jnp.einsum("sabd,lhde->lhsabe", jnp.asarray(row_hot), rpb.astype(jnp.float32) * LOG2E,
                   precision=lax.Precision.HIGHEST)
    g = g.reshape(nl, nh, 3, NA_QROWS, NA_KROWS // pair, pair * ne)
    lead = g.shape[:-1]
    g = jnp.concatenate([g, jnp.broadcast_to(jnp.asarray(row_bad), lead + (pair,)),
                         jnp.full(lead + (1,), NEG_INF, jnp.float32)], axis=-1)
    return jnp.einsum("lhsapk,kxz->lhsapxz", g, jnp.asarray(feats), precision=lax.Precision.HIGHEST)


def _sw_kernel(q_ref, k_ref, v_ref, *rest):
    *z_refs, bias_ref, sink_ref, o_ref = rest
    seq = k_ref.shape[0]
    blk = SW_BLOCK
    nblk = seq // blk
    gw = SW_GROUP * HEAD_DIM
    grows = SW_GROUP * blk

    for kv in range(SW_KV_PER_STEP):
        sk = sink_ref[kv * grows:(kv + 1) * grows, :]
        for i in range(nblk):
            r0 = i * blk
            lo = max(i - 1, 0) * blk
            hi = min(i + 2, nblk) * blk
            c0 = lo - (i - 1) * blk
            qs = jnp.concatenate(
                [q_ref[r0:r0 + blk, kv * gw + g * HEAD_DIM:kv * gw + (g + 1) * HEAD_DIM]
                 for g in range(SW_GROUP)], axis=0)
            kw = k_ref[lo:hi, kv * HEAD_DIM:(kv + 1) * HEAD_DIM]
            vw = v_ref[lo:hi, kv * HEAD_DIM:(kv + 1) * HEAD_DIM]
            s = (lax.dot_general(qs, kw, _NT_DIMS, preferred_element_type=jnp.float32)
                 + bias_ref[kv * grows:(kv + 1) * grows, c0:c0 + hi - lo])
            mx = jnp.maximum(jnp.max(s, axis=-1, keepdims=True), sk)
            mx_wide = jnp.concatenate([mx] * ((hi - lo) // HEAD_DIM), axis=1)
            num, tot = _weighted_sum_and_total(jnp.exp2(s - mx_wide).astype(vw.dtype), vw)
            o = num / (tot + jnp.exp2(sk - mx))
            og = jnp.concatenate([o[g * blk:(g + 1) * blk, :] for g in range(SW_GROUP)], axis=1)
            z = jnp.concatenate([r[r0:r0 + blk, :] for r in z_refs], axis=1)[:, kv * gw:(kv + 1) * gw]
            z = z.astype(jnp.float32)
            o_ref[r0:r0 + blk, kv * gw:(kv + 1) * gw] = (og * (z * _sigmoid(z))).astype(o_ref.dtype)


def _sw_attention(proj, bias, sink_col):
    bsz, seq, _ = proj.shape
    kvs = SW_KV_PER_STEP
    qw = kvs * SW_GROUP * HEAD_DIM
    kw = kvs * HEAD_DIM
    zw = math.gcd(OFF_ZB, qw)
    nz = qw // zw
    rows = kvs * SW_GROUP * SW_BLOCK
    assert SW_KV_HEADS % kvs == 0 and OFF_QB % qw == 0 and OFF_KB % kw == 0 and OFF_VB % kw == 0
    assert zw % LANES == 0

    def cols(off, width):
        return pl.BlockSpec((None, seq, width), lambda k, b: (b, 0, off // width + k))

    def gate(t):
        return pl.BlockSpec((None, seq, zw), lambda k, b: (b, 0, OFF_ZB // zw + k * nz + t))

    return pl.pallas_call(
        _sw_kernel,
        grid=(SW_KV_HEADS // kvs, bsz),
        in_specs=[cols(OFF_QB, qw), cols(OFF_KB, kw), cols(OFF_VB, kw)] + [gate(t) for t in range(nz)]
        + [pl.BlockSpec((rows, 3 * SW_BLOCK), lambda k, b: (k, 0)),
           pl.BlockSpec((rows, HEAD_DIM), lambda k, b: (k, 0))],
        out_specs=pl.BlockSpec((None, seq, qw), lambda k, b: (b, 0, k)),
        out_shape=jax.ShapeDtypeStruct((bsz, seq, SW_WIDTH), jnp.bfloat16),
        compiler_params=_params("parallel", "arbitrary"),
        name="sw_attention",
    )(*([proj] * (3 + nz)), bias, sink_col)


def _t5_bucket_index(rel):
    nb = T5_BUCKETS // 2
    max_exact = nb // 2
    ret = (rel > 0).astype(np.int32) * nb
    n = np.abs(rel)
    large = max_exact + (np.log(np.maximum(n, 1) / max_exact)
                         / np.log(T5_MAX_DIST / max_exact) * (nb - max_exact)).astype(np.int32)
    large = np.minimum(large, nb - 1)
    return (ret + np.where(n < max_exact, n, large)).astype(np.int32)


def _sw_bias_table(t5_bias):
    rel = np.arange(3 * SW_BLOCK)[None, :] - SW_BLOCK - np.arange(SW_BLOCK)[:, None]
    band = np.abs(rel) <= SW_WINDOW
    hot = (_t5_bucket_index(rel)[..., None] == np.arange(T5_BUCKETS)).astype(np.float32)
    t = jnp.einsum("qkb,bh->hqk", jnp.asarray(hot), t5_bias.astype(jnp.float32),
                   precision=lax.Precision.HIGHEST)
    t = jnp.where(jnp.asarray(band)[None], t * LOG2E, NEG_INF)
    return t.reshape(SW_HEADS * SW_BLOCK, 3 * SW_BLOCK)


def _mem_kernel(q_ref, z_ref, k_ref, v_ref, o_ref):
    seq = q_ref.shape[0]
    for hh in range(MEM_HEADS_PER_STEP):
        lanes = slice(hh * MEM_HEAD_DIM, (hh + 1) * MEM_HEAD_DIM)
        k = k_ref[:, lanes]
        v = v_ref[:, lanes]
        for i in range(seq // MEM_QBLOCK):
            r0 = i * MEM_QBLOCK
            q = q_ref[r0:r0 + MEM_QBLOCK, lanes]
            s = lax.dot_general(q, k, _NT_DIMS, preferred_element_type=jnp.float32)
            mx = jnp.max(s, axis=-1, keepdims=True)
            e = jnp.exp2(s - mx)
            den = jnp.sum(e, axis=-1, keepdims=True)
            o = jnp.dot(e.astype(v.dtype), v, preferred_element_type=jnp.float32) / den
            z = z_ref[r0:r0 + MEM_QBLOCK, lanes].astype(jnp.float32)
            o_ref[r0:r0 + MEM_QBLOCK, lanes] = (o * (z * _sigmoid(z))).astype(o_ref.dtype)


def _mem_attention(proj, memkv):
    bsz, seq, _ = proj.shape
    mtok = memkv.shape[1]

    width = MEM_HEADS_PER_STEP * MEM_HEAD_DIM

    def qcol(off):
        return pl.BlockSpec((None, seq, width), lambda h, b: (b, 0, off // width + h))

    def kvcol(off):
        return pl.BlockSpec((None, mtok, width), lambda h, b: (b, 0, off // width + h))

    assert MEM_HEADS % MEM_HEADS_PER_STEP == 0
    assert OFF_QM % width == 0 and OFF_ZM % width == 0 and MEM_WIDTH % width == 0
    return pl.pallas_call(
        _mem_kernel,
        grid=(MEM_HEADS // MEM_HEADS_PER_STEP, bsz),
        in_specs=[qcol(OFF_QM), qcol(OFF_ZM), kvcol(0), kvcol(MEM_WIDTH)],
        out_specs=pl.BlockSpec((None, seq, width), lambda h, b: (b, 0, h)),
        out_shape=jax.ShapeDtypeStruct((bsz, seq, MEM_WIDTH), jnp.bfloat16),
        compiler_params=_params("parallel", "arbitrary"),
        name="mem_attention",
    )(proj, proj, memkv, memkv)


def _merge_kernel(a_ref, b_ref, m_ref, wa_ref, wb_ref, wm_ref, ga_ref, gb_ref, gm_ref, o_ref):
    def branch(x_ref, w_ref, g_ref):
        y = jnp.dot(x_ref[...], w_ref[...], preferred_element_type=jnp.float32)
        return _sigmoid(g_ref[...].astype(jnp.float32)) * y

    acc = branch(a_ref, wa_ref, ga_ref)
    acc = acc + branch(b_ref, wb_ref, gb_ref)
    acc = acc + branch(m_ref, wm_ref, gm_ref)
    o_ref[...] = acc.astype(o_ref.dtype)


def _merge(a2d, b2d, m2d, wa, wb, wm, layer, proj2d):
    rows = a2d.shape[0]
    bm = min(MERGE_BM, rows)
    bn = MERGE_BN

    def act(width):
        return pl.BlockSpec((bm, width), lambda i, j: (i, 0))

    def wgt(width):
        return pl.BlockSpec((None, width, bn), lambda i, j: (layer, 0, j))

    def gate(off):
        return pl.BlockSpec((bm, bn), lambda i, j: (i, off // bn + j))

    assert OFF_GA % bn == 0 and OFF_GB % bn == 0 and OFF_GM % bn == 0 and rows % bm == 0
    return pl.pallas_call(
        _merge_kernel,
        grid=(rows // bm, D_MODEL // bn),
        in_specs=[act(NA_WIDTH), act(SW_WIDTH), act(MEM_WIDTH),
                  wgt(NA_WIDTH), wgt(SW_WIDTH), wgt(MEM_WIDTH),
                  gate(OFF_GA), gate(OFF_GB), gate(OFF_GM)],
        out_specs=pl.BlockSpec((bm, bn), lambda i, j: (i, j)),
        out_shape=jax.ShapeDtypeStruct((rows, D_MODEL), jnp.bfloat16),
        compiler_params=_params("parallel", "arbitrary"),
        name="branch_merge",
    )(a2d, b2d, m2d, wa, wb, wm, proj2d, proj2d, proj2d)


def _outproj_kernel(*refs, emit_next):
    if emit_next:
        mg_ref, w_ref, x_ref, g_ref, gn_ref, o_ref, xg_ref, ssqn_ref, y_ref, ssq_ref, ssqn_acc = refs
    else:
        mg_ref, w_ref, x_ref, g_ref, o_ref, y_ref, ssq_ref = refs
    i = pl.program_id(0)
    j = pl.program_id(1)
    n_tiles = pl.num_programs(0) - 1
    bn = w_ref.shape[1]
    d = y_ref.shape[1]
    col = pl.multiple_of(j * bn, bn)
    cur = lax.rem(i, 2)

    def finish():
        inv = lax.rsqrt(ssq_ref[1 - cur] * (1.0 / d) + RMS_EPS)
        x_new = x_ref[...] + (y_ref[:, pl.ds(col, bn)] * inv) * g_ref[...]
        o_ref[...] = x_new
        if emit_next:
            xg_ref[...] = (x_new * gn_ref[...]).astype(xg_ref.dtype)
            total = jnp.sum(x_new * x_new, axis=-1, keepdims=True) + jnp.where(j == 0, 0.0, ssqn_acc[...])
            ssqn_acc[...] = total
            ssqn_ref[...] = total

    def project():
        y = jnp.dot(mg_ref[...], w_ref[...], preferred_element_type=jnp.float32)
        y_ref[:, pl.ds(col, bn)] = y
        part = jnp.sum(y * y, axis=-1, keepdims=True)
        ssq_ref[cur] = part + jnp.where(j == 0, 0.0, ssq_ref[cur])

    @pl.when(i == 0)
    def _():
        @pl.when(j == 0)
        def _():
            ssq_ref[...] = jnp.zeros_like(ssq_ref)
            if emit_next:
                ssqn_acc[...] = jnp.zeros_like(ssqn_acc)
        project()

    @pl.when((i > 0) & (i < n_tiles))
    def _():
        finish()
        project()

    @pl.when(i == n_tiles)
    def _():
        finish()


def _outproj(merged2d, w_all, layer, x2d, gain, next_gain=None):
    rows, d = x2d.shape
    bm = min(OUT_BM, rows)
    bn = OUT_BN
    n_tiles = rows // bm

    def finished_block(i, j):
        return (jnp.maximum(i - 1, 0), jnp.where(i == 0, 0, j))

    emit_next = next_gain is not None
    operands = [merged2d, w_all, x2d, gain.reshape(1, d)]
    in_specs = [pl.BlockSpec((bm, d), lambda i, j: (jnp.minimum(i, n_tiles - 1), 0)),
                pl.BlockSpec((None, d, bn), lambda i, j: (layer, 0, j)),
                pl.BlockSpec((bm, bn), finished_block),
                pl.BlockSpec((1, bn), lambda i, j: (0, j))]
    out_specs = [pl.BlockSpec((bm, bn), finished_block)]
    out_shape = [jax.ShapeDtypeStruct((rows, d), jnp.float32)]
    if emit_next:
        operands.append(next_gain.reshape(1, d))
        in_specs.append(pl.BlockSpec((1, bn), lambda i, j: (0, j)))
        out_specs += [pl.BlockSpec((bm, bn), finished_block),
                      pl.BlockSpec((bm, 1), lambda i, j: (jnp.maximum(i - 1, 0), 0))]
        out_shape += [jax.ShapeDtypeStruct((rows, d), jnp.bfloat16),
                      jax.ShapeDtypeStruct((rows, 1), jnp.float32)]
    outs = pl.pallas_call(
        functools.partial(_outproj_kernel, emit_next=emit_next),
        grid=(n_tiles + 1, d // bn),
        in_specs=in_specs, out_specs=out_specs, out_shape=out_shape,
        scratch_shapes=[pltpu.VMEM((bm, d), jnp.float32),
                        pltpu.VMEM((2, bm, 1), jnp.float32)]
        + ([pltpu.VMEM((bm, 1), jnp.float32)] if emit_next else []),
        compiler_params=_params("arbitrary", "arbitrary"),
        name="out_proj_norm",
    )(*operands)
    return tuple(outs) if emit_next else outs[0]


_SIDE_CAST_WEIGHTS = ("w_mem_kv", "w_branch_a", "w_branch_b", "w_branch_m", "w_out")


def _in_proj_scale():
    s = np.ones((1, IN_WIDTH), np.float32)
    s[0, OFF_QA:OFF_QA + NA_WIDTH] = HEAD_DIM ** -0.5 * LOG2E
    s[0, OFF_QB:OFF_QB + SW_WIDTH] = HEAD_DIM ** -0.5 * LOG2E
    s[0, OFF_QM:OFF_QM + MEM_WIDTH] = MEM_HEAD_DIM ** -0.5 * LOG2E
    return s


def _trunk(x, mem, p):
    bsz, seq, d = x.shape
    mtok = mem.shape[1]
    x2d = x.reshape(bsz * seq, d)
    mem2d = mem.reshape(bsz * mtok, d)
    h, h_ssq = _rmsnorm(x2d, p["pre_norm"][0]), None
    for l in range(DEPTH):
        w_in_bf = p["w_in_bf16"]
        pending = [name for name in _SIDE_CAST_WEIGHTS if p[name].dtype != jnp.bfloat16]
        next_w_in = l + 1 < DEPTH and w_in_bf[l + 1] is None
        outs = _matmul(h, w_in_bf[l], 0, p["in_scale"], row_ssq=h_ssq,
                       cast_src=p["w_in"] if next_w_in else None, cast_layer=l + 1,
                       cast_all=[p[name] for name in pending])
        if next_w_in or pending:
            proj2d, copies = outs[0], list(outs[1:])
            if next_w_in:
                w_in_bf[l + 1] = copies.pop(0)
            p.update(zip(pending, copies))
        else:
            proj2d = outs
        proj = proj2d.reshape(bsz, seq, IN_WIDTH)
        mem_h = _rmsnorm(mem2d, p["mem_norm"][l])
        memkv = _matmul(mem_h, p["w_mem_kv"], l, p["kv_scale"]).reshape(bsz, mtok, 2 * MEM_WIDTH)
        a = _na_attention(proj, p["na_bias"], l)
        b = _sw_attention(proj, p["sw_bias"], p["sink_col"][l])
        m = _mem_attention(proj, memkv)
        merged = _merge(a.reshape(bsz * seq, NA_WIDTH), b.reshape(bsz * seq, SW_WIDTH),
                        m.reshape(bsz * seq, MEM_WIDTH), p["w_branch_a"], p["w_branch_b"],
                        p["w_branch_m"], l, proj2d)
        if l + 1 < DEPTH:
            x2d, h, h_ssq = _outproj(merged, p["w_out"], l, x2d, p["post_norm"][l],
                                     next_gain=p["pre_norm"][l + 1])
        else:
            x2d = _outproj(merged, p["w_out"], l, x2d, p["post_norm"][l])
    return x2d.reshape(bsz, seq, d)


def kernel(x_prompt, x_sample, mem_prompt, mem_sample, pre_norm, post_norm, mem_norm, w_in, w_mem_kv,
           w_branch_a, w_branch_b, w_branch_m, w_out, na_rpb, attn_sink, t5_bias):
    bf16 = jnp.bfloat16
    seq = x_prompt.shape[1]
    assert x_sample.shape[1] == seq and seq % (NA_QROWS * GRID_W) == 0 and seq % SW_BLOCK == 0
    assert seq // GRID_W >= NA_KROWS and seq % MEM_QBLOCK == 0 and seq // SW_BLOCK >= 2
    assert LANES % GRID_W == 0 and NA_KROWS % (LANES // GRID_W) == 0
    sink_col = jnp.repeat(attn_sink.astype(jnp.float32) * LOG2E, SW_BLOCK, axis=1)
    sink_col = jnp.broadcast_to(sink_col[..., None], sink_col.shape + (HEAD_DIM,))
    p = {
        "pre_norm": pre_norm, "post_norm": post_norm, "mem_norm": mem_norm,
        "w_in": w_in, "w_in_bf16": [_cast_layer(w_in, 0)] + [None] * (DEPTH - 1),
        "w_mem_kv": w_mem_kv, "w_branch_a": w_branch_a, "w_branch_b": w_branch_b,
        "w_branch_m": w_branch_m, "w_out": w_out,
        "in_scale": jnp.asarray(_in_proj_scale()),
        "kv_scale": jnp.ones((1, 2 * MEM_WIDTH), jnp.float32),
        "na_bias": _na_bias_tables(na_rpb, seq),
        "sw_bias": _sw_bias_table(t5_bias),
        "sink_col": sink_col,
    }
    y_prompt = _trunk(x_prompt, mem_prompt, p)
    y_sample = _trunk(x_sample, mem_sample, p)
    return (y_prompt, y_sample)
```

```python
import functools
import math

import numpy as np
import jax
import jax.numpy as jnp
from jax import lax
from jax.experimental import pallas as pl
from jax.experimental.pallas import tpu as pltpu

D_MODEL = 4096
DEPTH = 2
GRID_W = 64
HEAD_DIM = 128
NA_HEADS = 12
NA_ROWS = 8
NA_COLS = 16
SW_HEADS = 12
SW_KV_HEADS = 4
SW_GROUP = SW_HEADS // SW_KV_HEADS
SW_WINDOW = 128
SW_BLOCK = 128
T5_BUCKETS = 32
T5_MAX_DIST = 128
MEM_HEADS = 4
MEM_HEAD_DIM = 256
NA_WIDTH = NA_HEADS * HEAD_DIM
SW_WIDTH = SW_HEADS * HEAD_DIM
SW_KV_WIDTH = SW_KV_HEADS * HEAD_DIM
MEM_WIDTH = MEM_HEADS * MEM_HEAD_DIM
IN_SPLITS = (NA_WIDTH, NA_WIDTH, NA_WIDTH, NA_WIDTH,
             SW_WIDTH, SW_KV_WIDTH, SW_KV_WIDTH, SW_WIDTH,
             MEM_WIDTH, MEM_WIDTH,
             D_MODEL, D_MODEL, D_MODEL)
IN_WIDTH = sum(IN_SPLITS)
(OFF_QA, OFF_KA, OFF_VA, OFF_ZA, OFF_QB, OFF_KB, OFF_VB, OFF_ZB,
 OFF_QM, OFF_ZM, OFF_GA, OFF_GB, OFF_GM) = [int(c) for c in np.cumsum((0,) + IN_SPLITS[:-1])]
RMS_EPS = 1e-6
NEG_INF = -1e30
LOG2E = float(np.log2(np.e))

LANES = 128
BF16_SUBLANES = 16
VMEM_LIMIT_BYTES = 60 * 1024 * 1024
NORM_ROWS = 512
CAST_ROWS = 512
CAST_COLS = 2048
NA_HEADS_PER_STEP = 4
SW_KV_PER_STEP = 2
MEM_HEADS_PER_STEP = 2
MM_BM = 1024
MM_BN = 1024
MM_BN_WIDE = 1536
MERGE_BM = 1024
MERGE_BN = 1024
OUT_BM = 1024
OUT_BN = 512
NA_QROWS = 4
NA_KROWS = 12
MEM_QBLOCK = 512

_NT_DIMS = (((1,), (1,)), ((), ()))


def _params(*semantics):
    return pltpu.CompilerParams(dimension_semantics=semantics, vmem_limit_bytes=VMEM_LIMIT_BYTES)


def _weighted_sum_and_total(e, v):
    hd = v.shape[1]
    r = jnp.dot(e, jnp.concatenate([v, jnp.ones_like(v)], axis=1), preferred_element_type=jnp.float32)
    return r[:, :hd], r[:, hd:]


def _sigmoid(x):
    return 0.5 * jnp.tanh(0.5 * x) + 0.5


def _rmsnorm_kernel(x_ref, g_ref, o_ref):
    x = x_ref[...]
    ms = jnp.mean(x * x, axis=-1, keepdims=True)
    o_ref[...] = ((x * lax.rsqrt(ms + RMS_EPS)) * g_ref[...]).astype(o_ref.dtype)


def _rmsnorm(x2d, gain):
    rows, d = x2d.shape
    return pl.pallas_call(
        _rmsnorm_kernel,
        grid=(rows // NORM_ROWS,),
        in_specs=[pl.BlockSpec((NORM_ROWS, d), lambda i: (i, 0)),
                  pl.BlockSpec((1, d), lambda i: (0, 0))],
        out_specs=pl.BlockSpec((NORM_ROWS, d), lambda i: (i, 0)),
        out_shape=jax.ShapeDtypeStruct((rows, d), jnp.bfloat16),
        compiler_params=_params("parallel"),
        name="rmsnorm",
    )(x2d, gain.reshape(1, d))


def _matmul_kernel(*refs, has_row_ssq, n_casts):
    x_ref, w_ref, s_ref = refs[:3]
    n_in = 3 + has_row_ssq + n_casts
    o_ref = refs[n_in]
    acc = jnp.dot(x_ref[...], w_ref[...], preferred_element_type=jnp.float32)
    if has_row_ssq:
        acc = acc * lax.rsqrt(refs[3][...] * (1.0 / x_ref.shape[1]) + RMS_EPS)
    o_ref[...] = (acc * s_ref[...]).astype(o_ref.dtype)
    for c in range(n_casts):
        dst = refs[n_in + 1 + c]
        dst[...] = refs[3 + has_row_ssq + c][...].astype(dst.dtype)


def _matmul(x2d, w_all, layer, col_scale, row_ssq=None, cast_src=None, cast_layer=None, cast_all=()):
    m, k = x2d.shape
    n = w_all.shape[2]
    bm = min(MM_BM, m)
    has_streams = cast_src is not None or len(cast_all) > 0
    bn = MM_BN_WIDE if (n % MM_BN_WIDE == 0 and not has_streams) else min(MM_BN, n)
    assert m % bm == 0 and n % bn == 0
    grid = (m // bm, n // bn)
    operands = [x2d, w_all, col_scale]
    in_specs = [pl.BlockSpec((bm, k), lambda i, j: (i, 0)),
                pl.BlockSpec((None, k, bn), lambda i, j: (layer, 0, j)),
                pl.BlockSpec((1, bn), lambda i, j: (0, j))]
    out_specs = [pl.BlockSpec((bm, bn), lambda i, j: (i, j))]
    out_shape = [jax.ShapeDtypeStruct((m, n), jnp.bfloat16)]
    if row_ssq is not None:
        operands.append(row_ssq)
        in_specs.append(pl.BlockSpec((bm, 1), lambda i, j: (i, 0)))
    if cast_src is not None:
        _, k2, n2 = cast_src.shape
        assert k2 % grid[0] == 0 and n2 % grid[1] == 0
        cb = (k2 // grid[0], n2 // grid[1])
        operands.append(cast_src)
        in_specs.append(pl.BlockSpec((None,) + cb, lambda i, j: (cast_layer, i, j)))
        out_specs.append(pl.BlockSpec((None,) + cb, lambda i, j: (0, i, j)))
        out_shape.append(jax.ShapeDtypeStruct((1, k2, n2), jnp.bfloat16))
    for src in cast_all:
        depth, r, c = src.shape
        col_split = math.gcd(grid[1], c // LANES)
        lay_split = grid[1] // col_split
        assert r % (grid[0] * BF16_SUBLANES) == 0 and lay_split >= depth
        blk = (None, r // grid[0], c // col_split)

        def block_index(i, j, lay_split=lay_split, depth=depth):
            return (jnp.minimum(j % lay_split, depth - 1), i, j // lay_split)

        operands.append(src)
        in_specs.append(pl.BlockSpec(blk, block_index))
        out_specs.append(pl.BlockSpec(blk, block_index))
        out_shape.append(jax.ShapeDtypeStruct(src.shape, jnp.bfloat16))
    n_casts = (cast_src is not None) + len(cast_all)
    outs = pl.pallas_call(
        functools.partial(_matmul_kernel, has_row_ssq=row_ssq is not None, n_casts=n_casts),
        grid=grid, in_specs=in_specs, out_specs=out_specs, out_shape=out_shape,
        compiler_params=_params("parallel", "arbitrary"), name="proj_matmul",
    )(*operands)
    return outs[0] if n_casts == 0 else tuple(outs)


def _cast_kernel(src_ref, dst_ref):
    dst_ref[...] = src_ref[...].astype(dst_ref.dtype)


def _cast_layer(w_all, layer):
    _, k, n = w_all.shape
    bk, bn = min(CAST_ROWS, k), min(CAST_COLS, n)
    assert k % bk == 0 and n % bn == 0
    return pl.pallas_call(
        _cast_kernel,
        grid=(k // bk, n // bn),
        in_specs=[pl.BlockSpec((None, bk, bn), lambda i, j: (layer, i, j))],
        out_specs=pl.BlockSpec((None, bk, bn), lambda i, j: (0, i, j)),
        out_shape=jax.ShapeDtypeStruct((1, k, n), jnp.bfloat16),
        compiler_params=_params("parallel", "parallel"),
        name="cast_weights",
    )(w_all)


def _na_kernel(q_ref, k_ref, v_ref, z_ref, bias_ref, o_ref):
    seq = q_ref.shape[0]
    qb = NA_QROWS * GRID_W
    kb = NA_KROWS * GRID_W
    nblk = seq // qb
    grid_rows = seq // GRID_W

    for hh in range(NA_HEADS_PER_STEP):
        lanes = slice(hh * HEAD_DIM, (hh + 1) * HEAD_DIM)
        for i in range(nblk):
            case = 0 if i == 0 else (2 if i == nblk - 1 else 1)
            q0 = i * qb
            k0 = min(max(NA_QROWS * i - NA_ROWS // 2, 0), grid_rows - NA_KROWS) * GRID_W
            q = q_ref[q0:q0 + qb, lanes]
            k = k_ref[k0:k0 + kb, lanes]
            v = v_ref[k0:k0 + kb, lanes]
            bias = jnp.concatenate(
                [jnp.concatenate([bias_ref[hh, case, a, p] for p in range(bias_ref.shape[3])], axis=1)
                 for a in range(bias_ref.shape[2])], axis=0)
            s = lax.dot_general(q, k, _NT_DIMS, preferred_element_type=jnp.float32) + bias
            mx = jnp.max(s, axis=-1, keepdims=True)
            num, tot = _weighted_sum_and_total(jnp.exp2(s - mx).astype(v.dtype), v)
            o = num / tot
            z = z_ref[q0:q0 + qb, lanes].astype(jnp.float32)
            o_ref[q0:q0 + qb, lanes] = (o * (z * _sigmoid(z))).astype(o_ref.dtype)


def _na_attention(proj, bias_all, layer):
    bsz, seq, _ = proj.shape

    hps = NA_HEADS_PER_STEP
    width = hps * HEAD_DIM
    assert NA_HEADS % hps == 0 and all(off % width == 0 for off in (OFF_QA, OFF_KA, OFF_VA, OFF_ZA))

    def col(off):
        return pl.BlockSpec((None, seq, width), lambda h, b: (b, 0, off // width + h))

    return pl.pallas_call(
        _na_kernel,
        grid=(NA_HEADS // hps, bsz),
        in_specs=[col(OFF_QA), col(OFF_KA), col(OFF_VA), col(OFF_ZA),
                  pl.BlockSpec((None, hps) + bias_all.shape[2:], lambda h, b: (layer, h, 0, 0, 0, 0, 0))],
        out_specs=pl.BlockSpec((None, seq, width), lambda h, b: (b, 0, h)),
        out_shape=jax.ShapeDtypeStruct((bsz, seq, NA_WIDTH), jnp.bfloat16),
        compiler_params=_params("parallel", "arbitrary"),
        name="na_attention",
    )(proj, proj, proj, proj, bias_all)


def _na_bias_tables(rpb, seq):
    grid_rows = seq // GRID_W
    nblk = grid_rows // NA_QROWS
    win_rows = min(NA_ROWS, grid_rows)
    qc = np.arange(GRID_W)
    cstart = np.clip(qc - NA_COLS // 2, 0, GRID_W - NA_COLS)
    kc = np.arange(GRID_W)
    col_ok = (kc[None, :] >= cstart[:, None]) & (kc[None, :] < cstart[:, None] + NA_COLS)
    dc = np.clip(kc[None, :] - qc[:, None], -(NA_COLS - 1), NA_COLS - 1) + NA_COLS - 1
    row_ok, dr = [], []
    for i in (0, 1, nblk - 1):
        qr = NA_QROWS * i + np.arange(NA_QROWS)
        kr = np.clip(NA_QROWS * i - NA_ROWS // 2, 0, grid_rows - NA_KROWS) + np.arange(NA_KROWS)
        rs = np.clip(qr - win_rows // 2, 0, grid_rows - win_rows)
        ok = (kr[None, :] >= rs[:, None]) & (kr[None, :] < rs[:, None] + win_rows)
        row_ok.append(ok)
        dr.append(np.clip(kr[None, :] - qr[:, None] + NA_ROWS - 1, 0, 2 * NA_ROWS - 2))
    row_ok = np.stack(row_ok)
    dr = np.stack(dr)
    nd, ne = 2 * NA_ROWS - 1, 2 * NA_COLS - 1
    row_hot = (dr[..., None] == np.arange(nd)).astype(np.float32)
    col_hot = (dc[..., None] == np.arange(ne)).astype(np.float32)
    pair = LANES // GRID_W
    pair_hot = np.zeros((pair, ne, GRID_W, pair, GRID_W), np.float32)
    for b in range(pair):
        pair_hot[b, :, :, b, :] = np.transpose(col_hot, (2, 0, 1))
    pair_hot = pair_hot.reshape(pair * ne, GRID_W, LANES)
    row_sel = np.zeros((pair, GRID_W, pair, GRID_W), np.float32)
    for b in range(pair):
        row_sel[b, :, b, :] = 1.0
    col_bad = np.broadcast_to((~col_ok)[:, None, :], (GRID_W, pair, GRID_W)).astype(np.float32)
    feats = np.concatenate([pair_hot, row_sel.reshape(pair, GRID_W, LANES),
                            col_bad.reshape(1, GRID_W, LANES)], axis=0)
    row_bad = (~row_ok).astype(np.float32).reshape(3, NA_QROWS, NA_KROWS // pair, pair) * NEG_INF
    nl, nh = rpb.shape[0], rpb.shape[1]
    g = jnp.einsum("sabd,lhde->lhsabe", jnp.asarray(row_hot), rpb.astype(jnp.float32) * LOG2E,
                   precision=lax.Precision.HIGHEST)
    g = g.reshape(nl, nh, 3, NA_QROWS, NA_KROWS // pair, pair * ne)
    lead = g.shape[:-1]
    g = jnp.concatenate([g, jnp.broadcast_to(jnp.asarray(row_bad), lead + (pair,)),
                         jnp.full(lead + (1,), NEG_INF, jnp.float32)], axis=-1)
    return jnp.einsum("lhsapk,kxz->lhsapxz", g, jnp.asarray(feats), precision=lax.Precision.HIGHEST)


def _sw_kernel(q_ref, k_ref, v_ref, *rest):
    *z_refs, bias_ref, sink_ref, o_ref = rest
    seq = k_ref.shape[0]
    blk = SW_BLOCK
    nblk = seq // blk
    gw = SW_GROUP * HEAD_DIM
    grows = SW_GROUP * blk

    for kv in range(SW_KV_PER_STEP):
        sk = sink_ref[kv * grows:(kv + 1) * grows, :]
        for i in range(nblk):
            r0 = i * blk
            lo = max(i - 1, 0) * blk
            hi = min(i + 2, nblk) * blk
            c0 = lo - (i - 1) * blk
            qs = jnp.concatenate(
                [q_ref[r0:r0 + blk, kv * gw + g * HEAD_DIM:kv * gw + (g + 1) * HEAD_DIM]
                 for g in range(SW_GROUP)], axis=0)
            kw = k_ref[lo:hi, kv * HEAD_DIM:(kv + 1) * HEAD_DIM]
            vw = v_ref[lo:hi, kv * HEAD_DIM:(kv + 1) * HEAD_DIM]
            s = (lax.dot_general(qs, kw, _NT_DIMS, preferred_element_type=jnp.float32)
                 + bias_ref[kv * grows:(kv + 1) * grows, c0:c0 + hi - lo])
            mx = jnp.maximum(jnp.max(s, axis=-1, keepdims=True), sk)
            mx_wide = jnp.concatenate([mx] * ((hi - lo) // HEAD_DIM), axis=1)
            num, tot = _weighted_sum_and_total(jnp.exp2(s - mx_wide).astype(vw.dtype), vw)
            o = num / (tot + jnp.exp2(sk - mx))
            og = jnp.concatenate([o[g * blk:(g + 1) * blk, :] for g in range(SW_GROUP)], axis=1)
            z = jnp.concatenate([r[r0:r0 + blk, :] for r in z_refs], axis=1)[:, kv * gw:(kv + 1) * gw]
            z = z.astype(jnp.float32)
            o_ref[r0:r0 + blk, kv * gw:(kv + 1) * gw] = (og * (z * _sigmoid(z))).astype(o_ref.dtype)


def _sw_attention(proj, bias, sink_col):
    bsz, seq, _ = proj.shape
    kvs = SW_KV_PER_STEP
    qw = kvs * SW_GROUP * HEAD_DIM
    kw = kvs * HEAD_DIM
    zw = math.gcd(OFF_ZB, qw)
    nz = qw // zw
    rows = kvs * SW_GROUP * SW_BLOCK
    assert SW_KV_HEADS % kvs == 0 and OFF_QB % qw == 0 and OFF_KB % kw == 0 and OFF_VB % kw == 0
    assert zw % LANES == 0

    def cols(off, width):
        return pl.BlockSpec((None, seq, width), lambda k, b: (b, 0, off // width + k))

    def gate(t):
        return pl.BlockSpec((None, seq, zw), lambda k, b: (b, 0, OFF_ZB // zw + k * nz + t))

    return pl.pallas_call(
        _sw_kernel,
        grid=(SW_KV_HEADS // kvs, bsz),
        in_specs=[cols(OFF_QB, qw), cols(OFF_KB, kw), cols(OFF_VB, kw)] + [gate(t) for t in range(nz)]
        + [pl.BlockSpec((rows, 3 * SW_BLOCK), lambda k, b: (k, 0)),
           pl.BlockSpec((rows, HEAD_DIM), lambda k, b: (k, 0))],
        out_specs=pl.BlockSpec((None, seq, qw), lambda k, b: (b, 0, k)),
        out_shape=jax.ShapeDtypeStruct((bsz, seq, SW_WIDTH), jnp.bfloat16),
        compiler_params=_params("parallel", "arbitrary"),
        name="sw_attention",
    )(*([proj] * (3 + nz)), bias, sink_col)


def _t5_bucket_index(rel):
    nb = T5_BUCKETS // 2
    max_exact = nb // 2
    ret = (rel > 0).astype(np.int32) * nb
    n = np.abs(rel)
    large = max_exact + (np.log(np.maximum(n, 1) / max_exact)
                         / np.log(T5_MAX_DIST / max_exact) * (nb - max_exact)).astype(np.int32)
    large = np.minimum(large, nb - 1)
    return (ret + np.where(n < max_exact, n, large)).astype(np.int32)


def _sw_bias_table(t5_bias):
    rel = np.arange(3 * SW_BLOCK)[None, :] - SW_BLOCK - np.arange(SW_BLOCK)[:, None]
    band = np.abs(rel) <= SW_WINDOW
    hot = (_t5_bucket_index(rel)[..., None] == np.arange(T5_BUCKETS)).astype(np.float32)
    t = jnp.einsum("qkb,bh->hqk", jnp.asarray(hot), t5_bias.astype(jnp.float32),
                   precision=lax.Precision.HIGHEST)
    t = jnp.where(jnp.asarray(band)[None], t * LOG2E, NEG_INF)
    return t.reshape(SW_HEADS * SW_BLOCK, 3 * SW_BLOCK)


def _mem_kernel(q_ref, z_ref, k_ref, v_ref, o_ref):
    seq = q_ref.shape[0]
    for hh in range(MEM_HEADS_PER_STEP):
        lanes = slice(hh * MEM_HEAD_DIM, (hh + 1) * MEM_HEAD_DIM)
        k = k_ref[:, lanes]
        v = v_ref[:, lanes]
        for i in range(seq // MEM_QBLOCK):
            r0 = i * MEM_QBLOCK
            q = q_ref[r0:r0 + MEM_QBLOCK, lanes]
            s = lax.dot_general(q, k, _NT_DIMS, preferred_element_type=jnp.float32)
            mx = jnp.max(s, axis=-1, keepdims=True)
            e = jnp.exp2(s - mx)
            den = jnp.sum(e, axis=-1, keepdims=True)
            o = jnp.dot(e.astype(v.dtype), v, preferred_element_type=jnp.float32) / den
            z = z_ref[r0:r0 + MEM_QBLOCK, lanes].astype(jnp.float32)
            o_ref[r0:r0 + MEM_QBLOCK, lanes] = (o * (z * _sigmoid(z))).astype(o_ref.dtype)


def _mem_attention(proj, memkv):
    bsz, seq, _ = proj.shape
    mtok = memkv.shape[1]

    width = MEM_HEADS_PER_STEP * MEM_HEAD_DIM

    def qcol(off):
        return pl.BlockSpec((None, seq, width), lambda h, b: (b, 0, off // width + h))

    def kvcol(off):
        return pl.BlockSpec((None, mtok, width), lambda h, b: (b, 0, off // width + h))

    assert MEM_HEADS % MEM_HEADS_PER_STEP == 0
    assert OFF_QM % width == 0 and OFF_ZM % width == 0 and MEM_WIDTH % width == 0
    return pl.pallas_call(
        _mem_kernel,
        grid=(MEM_HEADS // MEM_HEADS_PER_STEP, bsz),
        in_specs=[qcol(OFF_QM), qcol(OFF_ZM), kvcol(0), kvcol(MEM_WIDTH)],
        out_specs=pl.BlockSpec((None, seq, width), lambda h, b: (b, 0, h)),
        out_shape=jax.ShapeDtypeStruct((bsz, seq, MEM_WIDTH), jnp.bfloat16),
        compiler_params=_params("parallel", "arbitrary"),
        name="mem_attention",
    )(proj, proj, memkv, memkv)


def _merge_kernel(a_ref, b_ref, m_ref, wa_ref, wb_ref, wm_ref, ga_ref, gb_ref, gm_ref, o_ref):
    def branch(x_ref, w_ref, g_ref):
        y = jnp.dot(x_ref[...], w_ref[...], preferred_element_type=jnp.float32)
        return _sigmoid(g_ref[...].astype(jnp.float32)) * y

    acc = branch(a_ref, wa_ref, ga_ref)
    acc = acc + branch(b_ref, wb_ref, gb_ref)
    acc = acc + branch(m_ref, wm_ref, gm_ref)
    o_ref[...] = acc.astype(o_ref.dtype)


def _merge(a2d, b2d, m2d, wa, wb, wm, layer, proj2d):
    rows = a2d.shape[0]
    bm = min(MERGE_BM, rows)
    bn = MERGE_BN

    def act(width):
        return pl.BlockSpec((bm, width), lambda i, j: (i, 0))

    def wgt(width):
        return pl.BlockSpec((None, width, bn), lambda i, j: (layer, 0, j))

    def gate(off):
        return pl.BlockSpec((bm, bn), lambda i, j: (i, off // bn + j))

    assert OFF_GA % bn == 0 and OFF_GB % bn == 0 and OFF_GM % bn == 0 and rows % bm == 0
    return pl.pallas_call(
        _merge_kernel,
        grid=(rows // bm, D_MODEL // bn),
        in_specs=[act(NA_WIDTH), act(SW_WIDTH), act(MEM_WIDTH),
                  wgt(NA_WIDTH), wgt(SW_WIDTH), wgt(MEM_WIDTH),
                  gate(OFF_GA), gate(OFF_GB), gate(OFF_GM)],
        out_specs=pl.BlockSpec((bm, bn), lambda i, j: (i, j)),
        out_shape=jax.ShapeDtypeStruct((rows, D_MODEL), jnp.bfloat16),
        compiler_params=_params("parallel", "arbitrary"),
        name="branch_merge",
    )(a2d, b2d, m2d, wa, wb, wm, proj2d, proj2d, proj2d)


def _outproj_kernel(*refs, emit_next):
    if emit_next:
        mg_ref, w_ref, x_ref, g_ref, gn_ref, o_ref, xg_ref, ssqn_ref, y_ref, ssq_ref, ssqn_acc = refs
    else:
        mg_ref, w_ref, x_ref, g_ref, o_ref, y_ref, ssq_ref = refs
    i = pl.program_id(0)
    j = pl.program_id(1)
    n_tiles = pl.num_programs(0) - 1
    bn = w_ref.shape[1]
    d = y_ref.shape[1]
    col = pl.multiple_of(j * bn, bn)
    cur = lax.rem(i, 2)

    def finish():
        inv = lax.rsqrt(ssq_ref[1 - cur] * (1.0 / d) + RMS_EPS)
        x_new = x_ref[...] + (y_ref[:, pl.ds(col, bn)] * inv) * g_ref[...]
        o_ref[...] = x_new
        if emit_next:
            xg_ref[...] = (x_new * gn_ref[...]).astype(xg_ref.dtype)
            total = jnp.sum(x_new * x_new, axis=-1, keepdims=True) + jnp.where(j == 0, 0.0, ssqn_acc[...])
            ssqn_acc[...] = total
            ssqn_ref[...] = total

    def project():
        y = jnp.dot(mg_ref[...], w_ref[...], preferred_element_type=jnp.float32)
        y_ref[:, pl.ds(col, bn)] = y
        part = jnp.sum(y * y, axis=-1, keepdims=True)
        ssq_ref[cur] = part + jnp.where(j == 0, 0.0, ssq_ref[cur])

    @pl.when(i == 0)
    def _():
        @pl.when(j == 0)
        def _():
            ssq_ref[...] = jnp.zeros_like(ssq_ref)
            if emit_next:
                ssqn_acc[...] = jnp.zeros_like(ssqn_acc)
        project()

    @pl.when((i > 0) & (i < n_tiles))
    def _():
        finish()
        project()

    @pl.when(i == n_tiles)
    def _():
        finish()


def _outproj(merged2d, w_all, layer, x2d, gain, next_gain=None):
    rows, d = x2d.shape
    bm = min(OUT_BM, rows)
    bn = OUT_BN
    n_tiles = rows // bm

    def finished_block(i, j):
        return (jnp.maximum(i - 1, 0), jnp.where(i == 0, 0, j))

    emit_next = next_gain is not None
    operands = [merged2d, w_all, x2d, gain.reshape(1, d)]
    in_specs = [pl.BlockSpec((bm, d), lambda i, j: (jnp.minimum(i, n_tiles - 1), 0)),
                pl.BlockSpec((None, d, bn), lambda i, j: (layer, 0, jnp.where(i == n_tiles, d // bn - 1, j))),
                pl.BlockSpec((bm, bn), finished_block),
                pl.BlockSpec((1, bn), lambda i, j: (0, j))]
    out_specs = [pl.BlockSpec((bm, bn), finished_block)]
    out_shape = [jax.ShapeDtypeStruct((rows, d), jnp.float32)]
    if emit_next:
        operands.append(next_gain.reshape(1, d))
        in_specs.append(pl.BlockSpec((1, bn), lambda i, j: (0, j)))
        out_specs += [pl.BlockSpec((bm, bn), finished_block),
                      pl.BlockSpec((bm, 1), lambda i, j: (jnp.maximum(i - 1, 0), 0))]
        out_shape += [jax.ShapeDtypeStruct((rows, d), jnp.bfloat16),
                      jax.ShapeDtypeStruct((rows, 1), jnp.float32)]
    outs = pl.pallas_call(
        functools.partial(_outproj_kernel, emit_next=emit_next),
        grid=(n_tiles + 1, d // bn),
        in_specs=in_specs, out_specs=out_specs, out_shape=out_shape,
        scratch_shapes=[pltpu.VMEM((bm, d), jnp.float32),
                        pltpu.VMEM((2, bm, 1), jnp.float32)]
        + ([pltpu.VMEM((bm, 1), jnp.float32)] if emit_next else []),
        compiler_params=_params("arbitrary", "arbitrary"),
        name="out_proj_norm",
    )(*operands)
    return tuple(outs) if emit_next else outs[0]


_SIDE_CAST_WEIGHTS = ("w_mem_kv", "w_branch_a", "w_branch_b", "w_branch_m", "w_out")


def _in_proj_scale():
    s = np.ones((1, IN_WIDTH), np.float32)
    s[0, OFF_QA:OFF_QA + NA_WIDTH] = HEAD_DIM ** -0.5 * LOG2E
    s[0, OFF_QB:OFF_QB + SW_WIDTH] = HEAD_DIM ** -0.5 * LOG2E
    s[0, OFF_QM:OFF_QM + MEM_WIDTH] = MEM_HEAD_DIM ** -0.5 * LOG2E
    return s


def _trunk(x, mem, p):
    bsz, seq, d = x.shape
    mtok = mem.shape[1]
    x2d = x.reshape(bsz * seq, d)
    mem2d = mem.reshape(bsz * mtok, d)
    h, h_ssq = _rmsnorm(x2d, p["pre_norm"][0]), None
    for l in range(DEPTH):
        w_in_bf = p["w_in_bf16"]
        pending = [name for name in _SIDE_CAST_WEIGHTS if p[name].dtype != jnp.bfloat16]
        next_w_in = l + 1 < DEPTH and w_in_bf[l + 1] is None
        outs = _matmul(h, w_in_bf[l], 0, p["in_scale"], row_ssq=h_ssq,
                       cast_src=p["w_in"] if next_w_in else None, cast_layer=l + 1,
                       cast_all=[p[name] for name in pending])
        if next_w_in or pending:
            proj2d, copies = outs[0], list(outs[1:])
            if next_w_in:
                w_in_bf[l + 1] = copies.pop(0)
            p.update(zip(pending, copies))
        else:
            proj2d = outs
        proj = proj2d.reshape(bsz, seq, IN_WIDTH)
        mem_h = _rmsnorm(mem2d, p["mem_norm"][l])
        memkv = _matmul(mem_h, p["w_mem_kv"], l, p["kv_scale"]).reshape(bsz, mtok, 2 * MEM_WIDTH)
        a = _na_attention(proj, p["na_bias"], l)
        b = _sw_attention(proj, p["sw_bias"], p["sink_col"][l])
        m = _mem_attention(proj, memkv)
        merged = _merge(a.reshape(bsz * seq, NA_WIDTH), b.reshape(bsz * seq, SW_WIDTH),
                        m.reshape(bsz * seq, MEM_WIDTH), p["w_branch_a"], p["w_branch_b"],
                        p["w_branch_m"], l, proj2d)
        if l + 1 < DEPTH:
            x2d, h, h_ssq = _outproj(merged, p["w_out"], l, x2d, p["post_norm"][l],
                                     next_gain=p["pre_norm"][l + 1])
        else:
            x2d = _outproj(merged, p["w_out"], l, x2d, p["post_norm"][l])
    return x2d.reshape(bsz, seq, d)


def kernel(x_prompt, x_sample, mem_prompt, mem_sample, pre_norm, post_norm, mem_norm, w_in, w_mem_kv,
           w_branch_a, w_branch_b, w_branch_m, w_out, na_rpb, attn_sink, t5_bias):
    bf16 = jnp.bfloat16
    seq = x_prompt.shape[1]
    assert x_sample.shape[1] == seq and seq % (NA_QROWS * GRID_W) == 0 and seq % SW_BLOCK == 0
    assert seq // GRID_W >= NA_KROWS and seq % MEM_QBLOCK == 0 and seq // SW_BLOCK >= 2
    assert LANES % GRID_W == 0 and NA_KROWS % (LANES // GRID_W) == 0
    sink_col = jnp.repeat(attn_sink.astype(jnp.float32) * LOG2E, SW_BLOCK, axis=1)
    sink_col = jnp.broadcast_to(sink_col[..., None], sink_col.shape + (HEAD_DIM,))
    p = {
        "pre_norm": pre_norm, "post_norm": post_norm, "mem_norm": mem_norm,
        "w_in": w_in, "w_in_bf16": [_cast_layer(w_in, 0)] + [None] * (DEPTH - 1),
        "w_mem_kv": w_mem_kv, "w_branch_a": w_branch_a, "w_branch_b": w_branch_b,
        "w_branch_m": w_branch_m, "w_out": w_out,
        "in_scale": jnp.asarray(_in_proj_scale()),
        "kv_scale": jnp.ones((1, 2 * MEM_WIDTH), jnp.float32),
        "na_bias": _na_bias_tables(na_rpb, seq),
        "sw_bias": _sw_bias_table(t5_bias),
        "sink_col": sink_col,
    }
    y_prompt = _trunk(x_prompt, mem_prompt, p)
    y_sample = _trunk(x_sample, mem_sample, p)
    return (y_prompt, y_sample)
```

```python
import functools
import math

import numpy as np
import jax
import jax.numpy as jnp
from jax import lax
from jax.experimental import pallas as pl
from jax.experimental.pallas import tpu as pltpu

D_MODEL = 4096
DEPTH = 2
GRID_W = 64
HEAD_DIM = 128
NA_HEADS = 12
NA_ROWS = 8
NA_COLS = 16
SW_HEADS = 12
SW_KV_HEADS = 4
SW_GROUP = SW_HEADS // SW_KV_HEADS
SW_WINDOW = 128
SW_BLOCK = 128
T5_BUCKETS = 32
T5_MAX_DIST = 128
MEM_HEADS = 4
MEM_HEAD_DIM = 256
NA_WIDTH = NA_HEADS * HEAD_DIM
SW_WIDTH = SW_HEADS * HEAD_DIM
SW_KV_WIDTH = SW_KV_HEADS * HEAD_DIM
MEM_WIDTH = MEM_HEADS * MEM_HEAD_DIM
IN_SPLITS = (NA_WIDTH, NA_WIDTH, NA_WIDTH, NA_WIDTH,
             SW_WIDTH, SW_KV_WIDTH, SW_KV_WIDTH, SW_WIDTH,
             MEM_WIDTH, MEM_WIDTH,
             D_MODEL, D_MODEL, D_MODEL)
IN_WIDTH = sum(IN_SPLITS)
(OFF_QA, OFF_KA, OFF_VA, OFF_ZA, OFF_QB, OFF_KB, OFF_VB, OFF_ZB,
 OFF_QM, OFF_ZM, OFF_GA, OFF_GB, OFF_GM) = [int(c) for c in np.cumsum((0,) + IN_SPLITS[:-1])]
RMS_EPS = 1e-6
NEG_INF = -1e30
LOG2E = float(np.log2(np.e))

LANES = 128
BF16_SUBLANES = 16
VMEM_LIMIT_BYTES = 60 * 1024 * 1024
NORM_ROWS = 512
CAST_ROWS = 512
CAST_COLS = 2048
NA_HEADS_PER_STEP = 4
SW_KV_PER_STEP = 2
MEM_HEADS_PER_STEP = 2
MM_BM = 1024
MM_BN = 1024
MM_BN_WIDE = 1536
MERGE_BM = 1024
MERGE_BN = 1024
OUT_BM = 1024
OUT_BN = 512
NA_QROWS = 4
NA_KROWS = 12
MEM_QBLOCK = 512

_NT_DIMS = (((1,), (1,)), ((), ()))


def _params(*semantics):
    return pltpu.CompilerParams(dimension_semantics=semantics, vmem_limit_bytes=VMEM_LIMIT_BYTES)


def _weighted_sum_and_total(e, v):
    hd = v.shape[1]
    r = jnp.dot(e, jnp.concatenate([v, jnp.ones_like(v)], axis=1), preferred_element_type=jnp.float32)
    return r[:, :hd], r[:, hd:]


def _sigmoid(x):
    return 0.5 * jnp.tanh(0.5 * x) + 0.5


def _rmsnorm_kernel(x_ref, g_ref, o_ref):
    x = x_ref[...]
    ms = jnp.mean(x * x, axis=-1, keepdims=True)
    o_ref[...] = ((x * lax.rsqrt(ms + RMS_EPS)) * g_ref[...]).astype(o_ref.dtype)


def _rmsnorm(x2d, gain):
    rows, d = x2d.shape
    return pl.pallas_call(
        _rmsnorm_kernel,
        grid=(rows // NORM_ROWS,),
        in_specs=[pl.BlockSpec((NORM_ROWS, d), lambda i: (i, 0)),
                  pl.BlockSpec((1, d), lambda i: (0, 0))],
        out_specs=pl.BlockSpec((NORM_ROWS, d), lambda i: (i, 0)),
        out_shape=jax.ShapeDtypeStruct((rows, d), jnp.bfloat16),
        compiler_params=_params("parallel"),
        name="rmsnorm",
    )(x2d, gain.reshape(1, d))


def _matmul_kernel(*refs, has_row_ssq, n_casts, has_norm):
    x_ref, w_ref, s_ref = refs[:3]
    n_in = 3 + has_row_ssq + n_casts + 2 * has_norm
    o_ref = refs[n_in]
    if has_norm:
        _rmsnorm_kernel(refs[n_in - 2], refs[n_in - 1], refs[-1])
    acc = jnp.dot(x_ref[...], w_ref[...], preferred_element_type=jnp.float32)
    if has_row_ssq:
        acc = acc * lax.rsqrt(refs[3][...] * (1.0 / x_ref.shape[1]) + RMS_EPS)
    o_ref[...] = (acc * s_ref[...]).astype(o_ref.dtype)
    for c in range(n_casts):
        dst = refs[n_in + 1 + c]
        dst[...] = refs[3 + has_row_ssq + c][...].astype(dst.dtype)


def _matmul(x2d, w_all, layer, col_scale, row_ssq=None, cast_src=None, cast_layer=None, cast_all=(),
            norm_src=None):
    m, k = x2d.shape
    n = w_all.shape[2]
    bm = min(MM_BM, m)
    has_streams = cast_src is not None or len(cast_all) > 0 or norm_src is not None
    bn = MM_BN_WIDE if (n % MM_BN_WIDE == 0 and not has_streams) else min(MM_BN, n)
    assert m % bm == 0 and n % bn == 0
    grid = (m // bm, n // bn)
    operands = [x2d, w_all, col_scale]
    in_specs = [pl.BlockSpec((bm, k), lambda i, j: (i, 0)),
                pl.BlockSpec((None, k, bn), lambda i, j: (layer, 0, j)),
                pl.BlockSpec((1, bn), lambda i, j: (0, j))]
    out_specs = [pl.BlockSpec((bm, bn), lambda i, j: (i, j))]
    out_shape = [jax.ShapeDtypeStruct((m, n), jnp.bfloat16)]
    if row_ssq is not None:
        operands.append(row_ssq)
        in_specs.append(pl.BlockSpec((bm, 1), lambda i, j: (i, 0)))
    if cast_src is not None:
        _, k2, n2 = cast_src.shape
        assert k2 % grid[0] == 0 and n2 % grid[1] == 0
        cb = (k2 // grid[0], n2 // grid[1])
        operands.append(cast_src)
        in_specs.append(pl.BlockSpec((None,) + cb, lambda i, j: (cast_layer, i, j)))
        out_specs.append(pl.BlockSpec((None,) + cb, lambda i, j: (0, i, j)))
        out_shape.append(jax.ShapeDtypeStruct((1, k2, n2), jnp.bfloat16))
    for src in cast_all:
        depth, r, c = src.shape
        col_split = math.gcd(grid[1], c // LANES)
        lay_split = grid[1] // col_split
        assert r % (grid[0] * BF16_SUBLANES) == 0 and lay_split >= depth
        blk = (None, r // grid[0], c // col_split)

        def block_index(i, j, lay_split=lay_split, depth=depth):
            return (jnp.minimum(j % lay_split, depth - 1), i, j // lay_split)

        operands.append(src)
        in_specs.append(pl.BlockSpec(blk, block_index))
        out_specs.append(pl.BlockSpec(blk, block_index))
        out_shape.append(jax.ShapeDtypeStruct(src.shape, jnp.bfloat16))
    n_casts = (cast_src is not None) + len(cast_all)
    if norm_src is not None:
        rows_f32, gain = norm_src
        r, d = rows_f32.shape
        per_row = max(c for c in range(1, grid[1] + 1)
                      if grid[1] % c == 0 and r % (grid[0] * c * BF16_SUBLANES) == 0)
        share = grid[1] // per_row
        steps = grid[0] * per_row

        def row_block(i, j):
            return (i * per_row + j // share, 0)

        operands += [rows_f32, gain.reshape(1, d)]
        in_specs += [pl.BlockSpec((r // steps, d), row_block), pl.BlockSpec((1, d), lambda i, j: (0, 0))]
        out_specs.append(pl.BlockSpec((r // steps, d), row_block))
        out_shape.append(jax.ShapeDtypeStruct((r, d), jnp.bfloat16))
    outs = pl.pallas_call(
        functools.partial(_matmul_kernel, has_row_ssq=row_ssq is not None, n_casts=n_casts,
                          has_norm=norm_src is not None),
        grid=grid, in_specs=in_specs, out_specs=out_specs, out_shape=out_shape,
        compiler_params=_params("parallel", "arbitrary"), name="proj_matmul",
    )(*operands)
    return outs[0] if len(outs) == 1 else tuple(outs)


def _cast_kernel(src_ref, dst_ref):
    dst_ref[...] = src_ref[...].astype(dst_ref.dtype)


def _cast_layer(w_all, layer):
    _, k, n = w_all.shape
    bk, bn = min(CAST_ROWS, k), min(CAST_COLS, n)
    assert k % bk == 0 and n % bn == 0
    return pl.pallas_call(
        _cast_kernel,
        grid=(k // bk, n // bn),
        in_specs=[pl.BlockSpec((None, bk, bn), lambda i, j: (layer, i, j))],
        out_specs=pl.BlockSpec((None, bk, bn), lambda i, j: (0, i, j)),
        out_shape=jax.ShapeDtypeStruct((1, k, n), jnp.bfloat16),
        compiler_params=_params("parallel", "parallel"),
        name="cast_weights",
    )(w_all)


def _na_kernel(q_ref, k_ref, v_ref, z_ref, bias_ref, o_ref):
    seq = q_ref.shape[0]
    qb = NA_QROWS * GRID_W
    kb = NA_KROWS * GRID_W
    nblk = seq // qb
    grid_rows = seq // GRID_W

    for hh in range(NA_HEADS_PER_STEP):
        lanes = slice(hh * HEAD_DIM, (hh + 1) * HEAD_DIM)
        for i in range(nblk):
            case = 0 if i == 0 else (2 if i == nblk - 1 else 1)
            q0 = i * qb
            k0 = min(max(NA_QROWS * i - NA_ROWS // 2, 0), grid_rows - NA_KROWS) * GRID_W
            q = q_ref[q0:q0 + qb, lanes]
            k = k_ref[k0:k0 + kb, lanes]
            v = v_ref[k0:k0 + kb, lanes]
            bias = jnp.concatenate(
                [jnp.concatenate([bias_ref[hh, case, a, p] for p in range(bias_ref.shape[3])], axis=1)
                 for a in range(bias_ref.shape[2])], axis=0)
            s = lax.dot_general(q, k, _NT_DIMS, preferred_element_type=jnp.float32) + bias
            mx = jnp.max(s, axis=-1, keepdims=True)
            num, tot = _weighted_sum_and_total(jnp.exp2(s - mx).astype(v.dtype), v)
            o = num / tot
            z = z_ref[q0:q0 + qb, lanes].astype(jnp.float32)
            o_ref[q0:q0 + qb, lanes] = (o * (z * _sigmoid(z))).astype(o_ref.dtype)


def _na_attention(proj, bias_all, layer):
    bsz, seq, _ = proj.shape

    hps = NA_HEADS_PER_STEP
    width = hps * HEAD_DIM
    assert NA_HEADS % hps == 0 and all(off % width == 0 for off in (OFF_QA, OFF_KA, OFF_VA, OFF_ZA))

    def col(off):
        return pl.BlockSpec((None, seq, width), lambda h, b: (b, 0, off // width + h))

    return pl.pallas_call(
        _na_kernel,
        grid=(NA_HEADS // hps, bsz),
        in_specs=[col(OFF_QA), col(OFF_KA), col(OFF_VA), col(OFF_ZA),
                  pl.BlockSpec((None, hps) + bias_all.shape[2:], lambda h, b: (layer, h, 0, 0, 0, 0, 0))],
        out_specs=pl.BlockSpec((None, seq, width), lambda h, b: (b, 0, h)),
        out_shape=jax.ShapeDtypeStruct((bsz, seq, NA_WIDTH), jnp.bfloat16),
        compiler_params=_params("parallel", "arbitrary"),
        name="na_attention",
    )(proj, proj, proj, proj, bias_all)


def _na_bias_tables(rpb, seq):
    grid_rows = seq // GRID_W
    nblk = grid_rows // NA_QROWS
    win_rows = min(NA_ROWS, grid_rows)
    qc = np.arange(GRID_W)
    cstart = np.clip(qc - NA_COLS // 2, 0, GRID_W - NA_COLS)
    kc = np.arange(GRID_W)
    col_ok = (kc[None, :] >= cstart[:, None]) & (kc[None, :] < cstart[:, None] + NA_COLS)
    dc = np.clip(kc[None, :] - qc[:, None], -(NA_COLS - 1), NA_COLS - 1) + NA_COLS - 1
    row_ok, dr = [], []
    for i in (0, 1, nblk - 1):
        qr = NA_QROWS * i + np.arange(NA_QROWS)
        kr = np.clip(NA_QROWS * i - NA_ROWS // 2, 0, grid_rows - NA_KROWS) + np.arange(NA_KROWS)
        rs = np.clip(qr - win_rows // 2, 0, grid_rows - win_rows)
        ok = (kr[None, :] >= rs[:, None]) & (kr[None, :] < rs[:, None] + win_rows)
        row_ok.append(ok)
        dr.append(np.clip(kr[None, :] - qr[:, None] + NA_ROWS - 1, 0, 2 * NA_ROWS - 2))
    row_ok = np.stack(row_ok)
    dr = np.stack(dr)
    nd, ne = 2 * NA_ROWS - 1, 2 * NA_COLS - 1
    row_hot = (dr[..., None] == np.arange(nd)).astype(np.float32)
    col_hot = (dc[..., None] == np.arange(ne)).astype(np.float32)
    pair = LANES // GRID_W
    pair_hot = np.zeros((pair, ne, GRID_W, pair, GRID_W), np.float32)
    for b in range(pair):
        pair_hot[b, :, :, b, :] = np.transpose(col_hot, (2, 0, 1))
    pair_hot = pair_hot.reshape(pair * ne, GRID_W, LANES)
    row_sel = np.zeros((pair, GRID_W, pair, GRID_W), np.float32)
    for b in range(pair):
        row_sel[b, :, b, :] = 1.0
    col_bad = np.broadcast_to((~col_ok)[:, None, :], (GRID_W, pair, GRID_W)).astype(np.float32)
    feats = np.concatenate([pair_hot, row_sel.reshape(pair, GRID_W, LANES),
                            col_bad.reshape(1, GRID_W, LANES)], axis=0)
    row_bad = (~row_ok).astype(np.float32).reshape(3, NA_QROWS, NA_KROWS // pair, pair) * NEG_INF
    nl, nh = rpb.shape[0], rpb.shape[1]
    g = jnp.einsum("sabd,lhde->lhsabe", jnp.asarray(row_hot), rpb.astype(jnp.float32) * LOG2E,
                   precision=lax.Precision.HIGHEST)
    g = g.reshape(nl, nh, 3, NA_QROWS, NA_KROWS // pair, pair * ne)
    lead = g.shape[:-1]
    g = jnp.concatenate([g, jnp.broadcast_to(jnp.asarray(row_bad), lead + (pair,)),
                         jnp.full(lead + (1,), NEG_INF, jnp.float32)], axis=-1)
    return jnp.einsum("lhsapk,kxz->lhsapxz", g, jnp.asarray(feats), precision=lax.Precision.HIGHEST)


def _sw_kernel(q_ref, k_ref, v_ref, *rest):
    *z_refs, bias_ref, sink_ref, o_ref = rest
    seq = k_ref.shape[0]
    blk = SW_BLOCK
    nblk = seq // blk
    gw = SW_GROUP * HEAD_DIM
    grows = SW_GROUP * blk

    for kv in range(SW_KV_PER_STEP):
        sk = sink_ref[kv * grows:(kv + 1) * grows, :]
        for i in range(nblk):
            r0 = i * blk
            lo = max(i - 1, 0) * blk
            hi = min(i + 2, nblk) * blk
            c0 = lo - (i - 1) * blk
            qs = jnp.concatenate(
                [q_ref[r0:r0 + blk, kv * gw + g * HEAD_DIM:kv * gw + (g + 1) * HEAD_DIM]
                 for g in range(SW_GROUP)], axis=0)
            kw = k_ref[lo:hi, kv * HEAD_DIM:(kv + 1) * HEAD_DIM]
            vw = v_ref[lo:hi, kv * HEAD_DIM:(kv + 1) * HEAD_DIM]
            s = (lax.dot_general(qs, kw, _NT_DIMS, preferred_element_type=jnp.float32)
                 + bias_ref[kv * grows:(kv + 1) * grows, c0:c0 + hi - lo])
            mx = jnp.maximum(jnp.max(s, axis=-1, keepdims=True), sk)
            mx_wide = jnp.concatenate([mx] * ((hi - lo) // HEAD_DIM), axis=1)
            num, tot = _weighted_sum_and_total(jnp.exp2(s - mx_wide).astype(vw.dtype), vw)
            o = num / (tot + jnp.exp2(sk - mx))
            og = jnp.concatenate([o[g * blk:(g + 1) * blk, :] for g in range(SW_GROUP)], axis=1)
            z = jnp.concatenate([r[r0:r0 + blk, :] for r in z_refs], axis=1)[:, kv * gw:(kv + 1) * gw]
            z = z.astype(jnp.float32)
            o_ref[r0:r0 + blk, kv * gw:(kv + 1) * gw] = (og * (z * _sigmoid(z))).astype(o_ref.dtype)


def _sw_attention(proj, bias, sink_col):
    bsz, seq, _ = proj.shape
    kvs = SW_KV_PER_STEP
    qw = kvs * SW_GROUP * HEAD_DIM
    kw = kvs * HEAD_DIM
    zw = math.gcd(OFF_ZB, qw)
    nz = qw // zw
    rows = kvs * SW_GROUP * SW_BLOCK
    assert SW_KV_HEADS % kvs == 0 and OFF_QB % qw == 0 and OFF_KB % kw == 0 and OFF_VB % kw == 0
    assert zw % LANES == 0

    def cols(off, width):
        return pl.BlockSpec((None, seq, width), lambda k, b: (b, 0, off // width + k))

    def gate(t):
        return pl.BlockSpec((None, seq, zw), lambda k, b: (b, 0, OFF_ZB // zw + k * nz + t))

    return pl.pallas_call(
        _sw_kernel,
        grid=(SW_KV_HEADS // kvs, bsz),
        in_specs=[cols(OFF_QB, qw), cols(OFF_KB, kw), cols(OFF_VB, kw)] + [gate(t) for t in range(nz)]
        + [pl.BlockSpec((rows, 3 * SW_BLOCK), lambda k, b: (k, 0)),
           pl.BlockSpec((rows, HEAD_DIM), lambda k, b: (k, 0))],
        out_specs=pl.BlockSpec((None, seq, qw), lambda k, b: (b, 0, k)),
        out_shape=jax.ShapeDtypeStruct((bsz, seq, SW_WIDTH), jnp.bfloat16),
        compiler_params=_params("parallel", "arbitrary"),
        name="sw_attention",
    )(*([proj] * (3 + nz)), bias, sink_col)


def _t5_bucket_index(rel):
    nb = T5_BUCKETS // 2
    max_exact = nb // 2
    ret = (rel > 0).astype(np.int32) * nb
    n = np.abs(rel)
    large = max_exact + (np.log(np.maximum(n, 1) / max_exact)
                         / np.log(T5_MAX_DIST / max_exact) * (nb - max_exact)).astype(np.int32)
    large = np.minimum(large, nb - 1)
    return (ret + np.where(n < max_exact, n, large)).astype(np.int32)


def _sw_bias_table(t5_bias):
    rel = np.arange(3 * SW_BLOCK)[None, :] - SW_BLOCK - np.arange(SW_BLOCK)[:, None]
    band = np.abs(rel) <= SW_WINDOW
    hot = (_t5_bucket_index(rel)[..., None] == np.arange(T5_BUCKETS)).astype(np.float32)
    t = jnp.einsum("qkb,bh->hqk", jnp.asarray(hot), t5_bias.astype(jnp.float32),
                   precision=lax.Precision.HIGHEST)
    t = jnp.where(jnp.asarray(band)[None], t * LOG2E, NEG_INF)
    return t.reshape(SW_HEADS * SW_BLOCK, 3 * SW_BLOCK)


def _mem_kernel(q_ref, z_ref, k_ref, v_ref, o_ref):
    seq = q_ref.shape[0]
    for hh in range(MEM_HEADS_PER_STEP):
        lanes = slice(hh * MEM_HEAD_DIM, (hh + 1) * MEM_HEAD_DIM)
        k = k_ref[:, lanes]
        v = v_ref[:, lanes]
        for i in range(seq // MEM_QBLOCK):
            r0 = i * MEM_QBLOCK
            q = q_ref[r0:r0 + MEM_QBLOCK, lanes]
            s = lax.dot_general(q, k, _NT_DIMS, preferred_element_type=jnp.float32)
            mx = jnp.max(s, axis=-1, keepdims=True)
            e = jnp.exp2(s - mx)
            den = jnp.sum(e, axis=-1, keepdims=True)
            o = jnp.dot(e.astype(v.dtype), v, preferred_element_type=jnp.float32) / den
            z = z_ref[r0:r0 + MEM_QBLOCK, lanes].astype(jnp.float32)
            o_ref[r0:r0 + MEM_QBLOCK, lanes] = (o * (z * _sigmoid(z))).astype(o_ref.dtype)


def _mem_attention(proj, memkv):
    bsz, seq, _ = proj.shape
    mtok = memkv.shape[1]

    width = MEM_HEADS_PER_STEP * MEM_HEAD_DIM

    def qcol(off):
        return pl.BlockSpec((None, seq, width), lambda h, b: (b, 0, off // width + h))

    def kvcol(off):
        return pl.BlockSpec((None, mtok, width), lambda h, b: (b, 0, off // width + h))

    assert MEM_HEADS % MEM_HEADS_PER_STEP == 0
    assert OFF_QM % width == 0 and OFF_ZM % width == 0 and MEM_WIDTH % width == 0
    return pl.pallas_call(
        _mem_kernel,
        grid=(MEM_HEADS // MEM_HEADS_PER_STEP, bsz),
        in_specs=[qcol(OFF_QM), qcol(OFF_ZM), kvcol(0), kvcol(MEM_WIDTH)],
        out_specs=pl.BlockSpec((None, seq, width), lambda h, b: (b, 0, h)),
        out_shape=jax.ShapeDtypeStruct((bsz, seq, MEM_WIDTH), jnp.bfloat16),
        compiler_params=_params("parallel", "arbitrary"),
        name="mem_attention",
    )(proj, proj, memkv, memkv)


def _merge_kernel(a_ref, b_ref, m_ref, wa_ref, wb_ref, wm_ref, ga_ref, gb_ref, gm_ref, o_ref):
    def branch(x_ref, w_ref, g_ref):
        y = jnp.dot(x_ref[...], w_ref[...], preferred_element_type=jnp.float32)
        return _sigmoid(g_ref[...].astype(jnp.float32)) * y

    acc = branch(a_ref, wa_ref, ga_ref)
    acc = acc + branch(b_ref, wb_ref, gb_ref)
    acc = acc + branch(m_ref, wm_ref, gm_ref)
    o_ref[...] = acc.astype(o_ref.dtype)


def _merge(a2d, b2d, m2d, wa, wb, wm, layer, proj2d):
    rows = a2d.shape[0]
    bm = min(MERGE_BM, rows)
    bn = MERGE_BN

    def act(width):
        return pl.BlockSpec((bm, width), lambda i, j: (i, 0))

    def wgt(width):
        return pl.BlockSpec((None, width, bn), lambda i, j: (layer, 0, j))

    def gate(off):
        return pl.BlockSpec((bm, bn), lambda i, j: (i, off // bn + j))

    assert OFF_GA % bn == 0 and OFF_GB % bn == 0 and OFF_GM % bn == 0 and rows % bm == 0
    return pl.pallas_call(
        _merge_kernel,
        grid=(rows // bm, D_MODEL // bn),
        in_specs=[act(NA_WIDTH), act(SW_WIDTH), act(MEM_WIDTH),
                  wgt(NA_WIDTH), wgt(SW_WIDTH), wgt(MEM_WIDTH),
                  gate(OFF_GA), gate(OFF_GB), gate(OFF_GM)],
        out_specs=pl.BlockSpec((bm, bn), lambda i, j: (i, j)),
        out_shape=jax.ShapeDtypeStruct((rows, D_MODEL), jnp.bfloat16),
        compiler_params=_params("parallel", "arbitrary"),
        name="branch_merge",
    )(a2d, b2d, m2d, wa, wb, wm, proj2d, proj2d, proj2d)


def _outproj_kernel(*refs, emit_next):
    if emit_next:
        mg_ref, w_ref, x_ref, g_ref, gn_ref, o_ref, xg_ref, ssqn_ref, y_ref, ssq_ref, ssqn_acc = refs
    else:
        mg_ref, w_ref, x_ref, g_ref, o_ref, y_ref, ssq_ref = refs
    i = pl.program_id(0)
    j = pl.program_id(1)
    n_tiles = pl.num_programs(0) - 1
    bn = w_ref.shape[1]
    d = y_ref.shape[1]
    col = pl.multiple_of(j * bn, bn)
    cur = lax.rem(i, 2)

    def finish():
        inv = lax.rsqrt(ssq_ref[1 - cur] * (1.0 / d) + RMS_EPS)
        x_new = x_ref[...] + (y_ref[:, pl.ds(col, bn)] * inv) * g_ref[...]
        o_ref[...] = x_new
        if emit_next:
            xg_ref[...] = (x_new * gn_ref[...]).astype(xg_ref.dtype)
            total = jnp.sum(x_new * x_new, axis=-1, keepdims=True) + jnp.where(j == 0, 0.0, ssqn_acc[...])
            ssqn_acc[...] = total
            ssqn_ref[...] = total

    def project():
        y = jnp.dot(mg_ref[...], w_ref[...], preferred_element_type=jnp.float32)
        y_ref[:, pl.ds(col, bn)] = y
        part = jnp.sum(y * y, axis=-1, keepdims=True)
        ssq_ref[cur] = part + jnp.where(j == 0, 0.0, ssq_ref[cur])

    @pl.when(i == 0)
    def _():
        @pl.when(j == 0)
        def _():
            ssq_ref[...] = jnp.zeros_like(ssq_ref)
            if emit_next:
                ssqn_acc[...] = jnp.zeros_like(ssqn_acc)
        project()

    @pl.when((i > 0) & (i < n_tiles))
    def _():
        finish()
        project()

    @pl.when(i == n_tiles)
    def _():
        finish()


def _outproj(merged2d, w_all, layer, x2d, gain, next_gain=None):
    rows, d = x2d.shape
    bm = min(OUT_BM, rows)
    bn = OUT_BN
    n_tiles = rows // bm

    def finished_block(i, j):
        return (jnp.maximum(i - 1, 0), jnp.where(i == 0, 0, j))

    emit_next = next_gain is not None
    operands = [merged2d, w_all, x2d, gain.reshape(1, d)]
    in_specs = [pl.BlockSpec((bm, d), lambda i, j: (jnp.minimum(i, n_tiles - 1), 0)),
                pl.BlockSpec((None, d, bn), lambda i, j: (layer, 0, jnp.where(i == n_tiles, d // bn - 1, j))),
                pl.BlockSpec((bm, bn), finished_block),
                pl.BlockSpec((1, bn), lambda i, j: (0, j))]
    out_specs = [pl.BlockSpec((bm, bn), finished_block)]
    out_shape = [jax.ShapeDtypeStruct((rows, d), jnp.float32)]
    if emit_next:
        operands.append(next_gain.reshape(1, d))
        in_specs.append(pl.BlockSpec((1, bn), lambda i, j: (0, j)))
        out_specs += [pl.BlockSpec((bm, bn), finished_block),
                      pl.BlockSpec((bm, 1), lambda i, j: (jnp.maximum(i - 1, 0), 0))]
        out_shape += [jax.ShapeDtypeStruct((rows, d), jnp.bfloat16),
                      jax.ShapeDtypeStruct((rows, 1), jnp.float32)]
    outs = pl.pallas_call(
        functools.partial(_outproj_kernel, emit_next=emit_next),
        grid=(n_tiles + 1, d // bn),
        in_specs=in_specs, out_specs=out_specs, out_shape=out_shape,
        scratch_shapes=[pltpu.VMEM((bm, d), jnp.float32),
                        pltpu.VMEM((2, bm, 1), jnp.float32)]
        + ([pltpu.VMEM((bm, 1), jnp.float32)] if emit_next else []),
        compiler_params=_params("arbitrary", "arbitrary"),
        name="out_proj_norm",
    )(*operands)
    return tuple(outs) if emit_next else outs[0]


_SIDE_CAST_WEIGHTS = ("w_mem_kv", "w_branch_a", "w_branch_b", "w_branch_m", "w_out")


def _in_proj_scale():
    s = np.ones((1, IN_WIDTH), np.float32)
    s[0, OFF_QA:OFF_QA + NA_WIDTH] = HEAD_DIM ** -0.5 * LOG2E
    s[0, OFF_QB:OFF_QB + SW_WIDTH] = HEAD_DIM ** -0.5 * LOG2E
    s[0, OFF_QM:OFF_QM + MEM_WIDTH] = MEM_HEAD_DIM ** -0.5 * LOG2E
    return s


def _trunk(x, mem, p, h0=None, norm_next=None):
    bsz, seq, d = x.shape
    mtok = mem.shape[1]
    x2d = x.reshape(bsz * seq, d)
    mem2d = mem.reshape(bsz * mtok, d)
    normed_next = None
    h, h_ssq = (_rmsnorm(x2d, p["pre_norm"][0]) if h0 is None else h0), None
    for l in range(DEPTH):
        w_in_bf = p["w_in_bf16"]
        pending = [name for name in _SIDE_CAST_WEIGHTS if p[name].dtype != jnp.bfloat16]
        next_w_in = l + 1 < DEPTH and w_in_bf[l + 1] is None
        side_norm = norm_next is not None and normed_next is None and not (next_w_in or pending)
        outs = _matmul(h, w_in_bf[l], 0, p["in_scale"], row_ssq=h_ssq,
                       cast_src=p["w_in"] if next_w_in else None, cast_layer=l + 1,
                       cast_all=[p[name] for name in pending],
                       norm_src=norm_next if side_norm else None)
        if next_w_in or pending or side_norm:
            proj2d, extras = outs[0], list(outs[1:])
            if next_w_in:
                w_in_bf[l + 1] = extras.pop(0)
            p.update(zip(pending, extras))
            if side_norm:
                normed_next = extras[-1]
        else:
            proj2d = outs
        proj = proj2d.reshape(bsz, seq, IN_WIDTH)
        mem_h = _rmsnorm(mem2d, p["mem_norm"][l])
        memkv = _matmul(mem_h, p["w_mem_kv"], l, p["kv_scale"]).reshape(bsz, mtok, 2 * MEM_WIDTH)
        a = _na_attention(proj, p["na_bias"], l)
        b = _sw_attention(proj, p["sw_bias"], p["sink_col"][l])
        m = _mem_attention(proj, memkv)
        merged = _merge(a.reshape(bsz * seq, NA_WIDTH), b.reshape(bsz * seq, SW_WIDTH),
                        m.reshape(bsz * seq, MEM_WIDTH), p["w_branch_a"], p["w_branch_b"],
                        p["w_branch_m"], l, proj2d)
        if l + 1 < DEPTH:
            x2d, h, h_ssq = _outproj(merged, p["w_out"], l, x2d, p["post_norm"][l],
                                     next_gain=p["pre_norm"][l + 1])
        else:
            x2d = _outproj(merged, p["w_out"], l, x2d, p["post_norm"][l])
    return x2d.reshape(bsz, seq, d), normed_next


def kernel(x_prompt, x_sample, mem_prompt, mem_sample, pre_norm, post_norm, mem_norm, w_in, w_mem_kv,
           w_branch_a, w_branch_b, w_branch_m, w_out, na_rpb, attn_sink, t5_bias):
    bf16 = jnp.bfloat16
    seq = x_prompt.shape[1]
    assert x_sample.shape[1] == seq and seq % (NA_QROWS * GRID_W) == 0 and seq % SW_BLOCK == 0
    assert seq // GRID_W >= NA_KROWS and seq % MEM_QBLOCK == 0 and seq // SW_BLOCK >= 2
    assert LANES % GRID_W == 0 and NA_KROWS % (LANES // GRID_W) == 0
    sink_col = jnp.repeat(attn_sink.astype(jnp.float32) * LOG2E, SW_BLOCK, axis=1)
    sink_col = jnp.broadcast_to(sink_col[..., None], sink_col.shape + (HEAD_DIM,))
    p = {
        "pre_norm": pre_norm, "post_norm": post_norm, "mem_norm": mem_norm,
        "w_in": w_in, "w_in_bf16": [_cast_layer(w_in, 0)] + [None] * (DEPTH - 1),
        "w_mem_kv": w_mem_kv, "w_branch_a": w_branch_a, "w_branch_b": w_branch_b,
        "w_branch_m": w_branch_m, "w_out": w_out,
        "in_scale": jnp.asarray(_in_proj_scale()),
        "kv_scale": jnp.ones((1, 2 * MEM_WIDTH), jnp.float32),
        "na_bias": _na_bias_tables(na_rpb, seq),
        "sw_bias": _sw_bias_table(t5_bias),
        "sink_col": sink_col,
    }
    x_sample2d = x_sample.reshape(-1, x_sample.shape[-1])
    y_prompt, h_sample = _trunk(x_prompt, mem_prompt, p, norm_next=(x_sample2d, pre_norm[0]))
    y_sample, _ = _trunk(x_sample, mem_sample, p, h0=h_sample)
    return (y_prompt, y_sample)
```

```python
import functools
import math

import numpy as np
import jax
import jax.numpy as jnp
from jax import lax
from jax.experimental import pallas as pl
from jax.experimental.pallas import tpu as pltpu

D_MODEL = 4096
DEPTH = 2
GRID_W = 64
HEAD_DIM = 128
NA_HEADS = 12
NA_ROWS = 8
NA_COLS = 16
SW_HEADS = 12
SW_KV_HEADS = 4
SW_GROUP = SW_HEADS // SW_KV_HEADS
SW_WINDOW = 128
SW_BLOCK = 128
T5_BUCKETS = 32
T5_MAX_DIST = 128
MEM_HEADS = 4
MEM_HEAD_DIM = 256
NA_WIDTH = NA_HEADS * HEAD_DIM
SW_WIDTH = SW_HEADS * HEAD_DIM
SW_KV_WIDTH = SW_KV_HEADS * HEAD_DIM
MEM_WIDTH = MEM_HEADS * MEM_HEAD_DIM
IN_SPLITS = (NA_WIDTH, NA_WIDTH, NA_WIDTH, NA_WIDTH,
             SW_WIDTH, SW_KV_WIDTH, SW_KV_WIDTH, SW_WIDTH,
             MEM_WIDTH, MEM_WIDTH,
             D_MODEL, D_MODEL, D_MODEL)
IN_WIDTH = sum(IN_SPLITS)
(OFF_QA, OFF_KA, OFF_VA, OFF_ZA, OFF_QB, OFF_KB, OFF_VB, OFF_ZB,
 OFF_QM, OFF_ZM, OFF_GA, OFF_GB, OFF_GM) = [int(c) for c in np.cumsum((0,) + IN_SPLITS[:-1])]
RMS_EPS = 1e-6
NEG_INF = -1e30
LOG2E = float(np.log2(np.e))

LANES = 128
BF16_SUBLANES = 16
VMEM_LIMIT_BYTES = 60 * 1024 * 1024
NORM_ROWS = 512
CAST_ROWS = 512
CAST_COLS = 2048
NA_HEADS_PER_STEP = 4
SW_KV_PER_STEP = 2
MEM_HEADS_PER_STEP = 2
MM_BM = 1024
MM_BN = 1024
MM_BN_WIDE = 1536
MERGE_BM = 1024
MERGE_BN = 1024
OUT_BM = 1024
OUT_BN = 512
NA_QROWS = 4
NA_KROWS = 12
MEM_QBLOCK = 512

_NT_DIMS = (((1,), (1,)), ((), ()))


def _params(*semantics):
    return pltpu.CompilerParams(dimension_semantics=semantics, vmem_limit_bytes=VMEM_LIMIT_BYTES)


def _weighted_sum_and_total(e, v):
    hd = v.shape[1]
    r = jnp.dot(e, jnp.concatenate([v, jnp.ones_like(v)], axis=1), preferred_element_type=jnp.float32)
    return r[:, :hd], r[:, hd:]


def _twice_sigmoid_of_double(xh):
    return jnp.tanh(xh) + 1.0


def _silu_of_double(zh):
    return zh * _twice_sigmoid_of_double(zh)


def _rmsnorm_kernel(x_ref, g_ref, o_ref):
    x = x_ref[...]
    ms = jnp.mean(x * x, axis=-1, keepdims=True)
    o_ref[...] = ((x * lax.rsqrt(ms + RMS_EPS)) * g_ref[...]).astype(o_ref.dtype)


def _rmsnorm(x2d, gain):
    rows, d = x2d.shape
    return pl.pallas_call(
        _rmsnorm_kernel,
        grid=(rows // NORM_ROWS,),
        in_specs=[pl.BlockSpec((NORM_ROWS, d), lambda i: (i, 0)),
                  pl.BlockSpec((1, d), lambda i: (0, 0))],
        out_specs=pl.BlockSpec((NORM_ROWS, d), lambda i: (i, 0)),
        out_shape=jax.ShapeDtypeStruct((rows, d), jnp.bfloat16),
        compiler_params=_params("parallel"),
        name="rmsnorm",
    )(x2d, gain.reshape(1, d))


def _matmul_kernel(*refs, has_row_ssq, n_casts):
    x_ref, w_ref, s_ref = refs[:3]
    n_in = 3 + has_row_ssq + n_casts
    o_ref = refs[n_in]
    acc = jnp.dot(x_ref[...], w_ref[...], preferred_element_type=jnp.float32)
    if has_row_ssq:
        acc = acc * lax.rsqrt(refs[3][...] * (1.0 / x_ref.shape[1]) + RMS_EPS)
    o_ref[...] = (acc * s_ref[...]).astype(o_ref.dtype)
    for c in range(n_casts):
        dst = refs[n_in + 1 + c]
        dst[...] = refs[3 + has_row_ssq + c][...].astype(dst.dtype)


def _matmul(x2d, w_all, layer, col_scale, row_ssq=None, cast_src=None, cast_layer=None, cast_all=()):
    m, k = x2d.shape
    n = w_all.shape[2]
    bm = min(MM_BM, m)
    has_streams = cast_src is not None or len(cast_all) > 0
    bn = MM_BN_WIDE if (n % MM_BN_WIDE == 0 and not has_streams) else min(MM_BN, n)
    assert m % bm == 0 and n % bn == 0
    grid = (m // bm, n // bn)
    operands = [x2d, w_all, col_scale]
    in_specs = [pl.BlockSpec((bm, k), lambda i, j: (i, 0)),
                pl.BlockSpec((None, k, bn), lambda i, j: (layer, 0, j)),
                pl.BlockSpec((1, bn), lambda i, j: (0, j))]
    out_specs = [pl.BlockSpec((bm, bn), lambda i, j: (i, j))]
    out_shape = [jax.ShapeDtypeStruct((m, n), jnp.bfloat16)]
    if row_ssq is not None:
        operands.append(row_ssq)
        in_specs.append(pl.BlockSpec((bm, 1), lambda i, j: (i, 0)))
    if cast_src is not None:
        _, k2, n2 = cast_src.shape
        assert k2 % grid[0] == 0 and n2 % grid[1] == 0
        cb = (k2 // grid[0], n2 // grid[1])
        operands.append(cast_src)
        in_specs.append(pl.BlockSpec((None,) + cb, lambda i, j: (cast_layer, i, j)))
        out_specs.append(pl.BlockSpec((None,) + cb, lambda i, j: (0, i, j)))
        out_shape.append(jax.ShapeDtypeStruct((1, k2, n2), jnp.bfloat16))
    for src in cast_all:
        depth, r, c = src.shape
        col_split = math.gcd(grid[1], c // LANES)
        lay_split = grid[1] // col_split
        assert r % (grid[0] * BF16_SUBLANES) == 0 and lay_split >= depth
        blk = (None, r // grid[0], c // col_split)

        def block_index(i, j, lay_split=lay_split, depth=depth):
            return (jnp.minimum(j % lay_split, depth - 1), i, j // lay_split)

        operands.append(src)
        in_specs.append(pl.BlockSpec(blk, block_index))
        out_specs.append(pl.BlockSpec(blk, block_index))
        out_shape.append(jax.ShapeDtypeStruct(src.shape, jnp.bfloat16))
    n_casts = (cast_src is not None) + len(cast_all)
    outs = pl.pallas_call(
        functools.partial(_matmul_kernel, has_row_ssq=row_ssq is not None, n_casts=n_casts),
        grid=grid, in_specs=in_specs, out_specs=out_specs, out_shape=out_shape,
        compiler_params=_params("parallel", "arbitrary"), name="proj_matmul",
    )(*operands)
    return outs[0] if n_casts == 0 else tuple(outs)


def _cast_kernel(src_ref, dst_ref):
    dst_ref[...] = src_ref[...].astype(dst_ref.dtype)


def _cast_layer(w_all, layer):
    _, k, n = w_all.shape
    bk, bn = min(CAST_ROWS, k), min(CAST_COLS, n)
    assert k % bk == 0 and n % bn == 0
    return pl.pallas_call(
        _cast_kernel,
        grid=(k // bk, n // bn),
        in_specs=[pl.BlockSpec((None, bk, bn), lambda i, j: (layer, i, j))],
        out_specs=pl.BlockSpec((None, bk, bn), lambda i, j: (0, i, j)),
        out_shape=jax.ShapeDtypeStruct((1, k, n), jnp.bfloat16),
        compiler_params=_params("parallel", "parallel"),
        name="cast_weights",
    )(w_all)


def _na_kernel(q_ref, k_ref, v_ref, z_ref, bias_ref, o_ref):
    seq = q_ref.shape[0]
    qb = NA_QROWS * GRID_W
    kb = NA_KROWS * GRID_W
    nblk = seq // qb
    grid_rows = seq // GRID_W

    for hh in range(NA_HEADS_PER_STEP):
        lanes = slice(hh * HEAD_DIM, (hh + 1) * HEAD_DIM)
        for i in range(nblk):
            case = 0 if i == 0 else (2 if i == nblk - 1 else 1)
            q0 = i * qb
            k0 = min(max(NA_QROWS * i - NA_ROWS // 2, 0), grid_rows - NA_KROWS) * GRID_W
            q = q_ref[q0:q0 + qb, lanes]
            k = k_ref[k0:k0 + kb, lanes]
            v = v_ref[k0:k0 + kb, lanes]
            bias = jnp.concatenate(
                [jnp.concatenate([bias_ref[hh, case, a, p] for p in range(bias_ref.shape[3])], axis=1)
                 for a in range(bias_ref.shape[2])], axis=0)
            s = lax.dot_general(q, k, _NT_DIMS, preferred_element_type=jnp.float32) + bias
            mx = jnp.max(s, axis=-1, keepdims=True)
            num, tot = _weighted_sum_and_total(jnp.exp2(s - mx).astype(v.dtype), v)
            o = num / tot
            z = z_ref[q0:q0 + qb, lanes].astype(jnp.float32)
            o_ref[q0:q0 + qb, lanes] = (o * _silu_of_double(z)).astype(o_ref.dtype)


def _na_attention(proj, bias_all, layer):
    bsz, seq, _ = proj.shape

    hps = NA_HEADS_PER_STEP
    width = hps * HEAD_DIM
    assert NA_HEADS % hps == 0 and all(off % width == 0 for off in (OFF_QA, OFF_KA, OFF_VA, OFF_ZA))

    def col(off):
        return pl.BlockSpec((None, seq, width), lambda h, b: (b, 0, off // width + h))

    return pl.pallas_call(
        _na_kernel,
        grid=(NA_HEADS // hps, bsz),
        in_specs=[col(OFF_QA), col(OFF_KA), col(OFF_VA), col(OFF_ZA),
                  pl.BlockSpec((None, hps) + bias_all.shape[2:], lambda h, b: (layer, h, 0, 0, 0, 0, 0))],
        out_specs=pl.BlockSpec((None, seq, width), lambda h, b: (b, 0, h)),
        out_shape=jax.ShapeDtypeStruct((bsz, seq, NA_WIDTH), jnp.bfloat16),
        compiler_params=_params("parallel", "arbitrary"),
        name="na_attention",
    )(proj, proj, proj, proj, bias_all)


def _na_bias_tables(rpb, seq):
    grid_rows = seq // GRID_W
    nblk = grid_rows // NA_QROWS
    win_rows = min(NA_ROWS, grid_rows)
    qc = np.arange(GRID_W)
    cstart = np.clip(qc - NA_COLS // 2, 0, GRID_W - NA_COLS)
    kc = np.arange(GRID_W)
    col_ok = (kc[None, :] >= cstart[:, None]) & (kc[None, :] < cstart[:, None] + NA_COLS)
    dc = np.clip(kc[None, :] - qc[:, None], -(NA_COLS - 1), NA_COLS - 1) + NA_COLS - 1
    row_ok, dr = [], []
    for i in (0, 1, nblk - 1):
        qr = NA_QROWS * i + np.arange(NA_QROWS)
        kr = np.clip(NA_QROWS * i - NA_ROWS // 2, 0, grid_rows - NA_KROWS) + np.arange(NA_KROWS)
        rs = np.clip(qr - win_rows // 2, 0, grid_rows - win_rows)
        ok = (kr[None, :] >= rs[:, None]) & (kr[None, :] < rs[:, None] + win_rows)
        row_ok.append(ok)
        dr.append(np.clip(kr[None, :] - qr[:, None] + NA_ROWS - 1, 0, 2 * NA_ROWS - 2))
    row_ok = np.stack(row_ok)
    dr = np.stack(dr)
    nd, ne = 2 * NA_ROWS - 1, 2 * NA_COLS - 1
    row_hot = (dr[..., None] == np.arange(nd)).astype(np.float32)
    col_hot = (dc[..., None] == np.arange(ne)).astype(np.float32)
    pair = LANES // GRID_W
    pair_hot = np.zeros((pair, ne, GRID_W, pair, GRID_W), np.float32)
    for b in range(pair):
        pair_hot[b, :, :, b, :] = np.transpose(col_hot, (2, 0, 1))
    pair_hot = pair_hot.reshape(pair * ne, GRID_W, LANES)
    row_sel = np.zeros((pair, GRID_W, pair, GRID_W), np.float32)
    for b in range(pair):
        row_sel[b, :, b, :] = 1.0
    col_bad = np.broadcast_to((~col_ok)[:, None, :], (GRID_W, pair, GRID_W)).astype(np.float32)
    feats = np.concatenate([pair_hot, row_sel.reshape(pair, GRID_W, LANES),
                            col_bad.reshape(1, GRID_W, LANES)], axis=0)
    row_bad = (~row_ok).astype(np.float32).reshape(3, NA_QROWS, NA_KROWS // pair, pair) * NEG_INF
    nl, nh = rpb.shape[0], rpb.shape[1]
    g = jnp.einsum("sabd,lhde->lhsabe", jnp.asarray(row_hot), rpb.astype(jnp.float32) * LOG2E,
                   precision=lax.Precision.HIGHEST)
    g = g.reshape(nl, nh, 3, NA_QROWS, NA_KROWS // pair, pair * ne)
    lead = g.shape[:-1]
    g = jnp.concatenate([g, jnp.broadcast_to(jnp.asarray(row_bad), lead + (pair,)),
                         jnp.full(lead + (1,), NEG_INF, jnp.float32)], axis=-1)
    return jnp.einsum("lhsapk,kxz->lhsapxz", g, jnp.asarray(feats), precision=lax.Precision.HIGHEST)


def _sw_kernel(q_ref, k_ref, v_ref, *rest):
    *z_refs, bias_ref, sink_ref, o_ref = rest
    seq = k_ref.shape[0]
    blk = SW_BLOCK
    nblk = seq // blk
    gw = SW_GROUP * HEAD_DIM
    grows = SW_GROUP * blk

    for kv in range(SW_KV_PER_STEP):
        sk = sink_ref[kv * grows:(kv + 1) * grows, :]
        for i in range(nblk):
            r0 = i * blk
            lo = max(i - 1, 0) * blk
            hi = min(i + 2, nblk) * blk
            c0 = lo - (i - 1) * blk
            qs = jnp.concatenate(
                [q_ref[r0:r0 + blk, kv * gw + g * HEAD_DIM:kv * gw + (g + 1) * HEAD_DIM]
                 for g in range(SW_GROUP)], axis=0)
            kw = k_ref[lo:hi, kv * HEAD_DIM:(kv + 1) * HEAD_DIM]
            vw = v_ref[lo:hi, kv * HEAD_DIM:(kv + 1) * HEAD_DIM]
            s = (lax.dot_general(qs, kw, _NT_DIMS, preferred_element_type=jnp.float32)
                 + bias_ref[kv * grows:(kv + 1) * grows, c0:c0 + hi - lo])
            mx = jnp.maximum(jnp.max(s, axis=-1, keepdims=True), sk)
            mx_wide = jnp.concatenate([mx] * ((hi - lo) // HEAD_DIM), axis=1)
            num, tot = _weighted_sum_and_total(jnp.exp2(s - mx_wide).astype(vw.dtype), vw)
            o = num / (tot + jnp.exp2(sk - mx))
            og = jnp.concatenate([o[g * blk:(g + 1) * blk, :] for g in range(SW_GROUP)], axis=1)
            z = jnp.concatenate([r[r0:r0 + blk, :] for r in z_refs], axis=1)[:, kv * gw:(kv + 1) * gw]
            z = z.astype(jnp.float32)
            o_ref[r0:r0 + blk, kv * gw:(kv + 1) * gw] = (og * _silu_of_double(z)).astype(o_ref.dtype)


def _sw_attention(proj, bias, sink_col):
    bsz, seq, _ = proj.shape
    kvs = SW_KV_PER_STEP
    qw = kvs * SW_GROUP * HEAD_DIM
    kw = kvs * HEAD_DIM
    zw = math.gcd(OFF_ZB, qw)
    nz = qw // zw
    rows = kvs * SW_GROUP * SW_BLOCK
    assert SW_KV_HEADS % kvs == 0 and OFF_QB % qw == 0 and OFF_KB % kw == 0 and OFF_VB % kw == 0
    assert zw % LANES == 0

    def cols(off, width):
        return pl.BlockSpec((None, seq, width), lambda k, b: (b, 0, off // width + k))

    def gate(t):
        return pl.BlockSpec((None, seq, zw), lambda k, b: (b, 0, OFF_ZB // zw + k * nz + t))

    return pl.pallas_call(
        _sw_kernel,
        grid=(SW_KV_HEADS // kvs, bsz),
        in_specs=[cols(OFF_QB, qw), cols(OFF_KB, kw), cols(OFF_VB, kw)] + [gate(t) for t in range(nz)]
        + [pl.BlockSpec((rows, 3 * SW_BLOCK), lambda k, b: (k, 0)),
           pl.BlockSpec((rows, HEAD_DIM), lambda k, b: (k, 0))],
        out_specs=pl.BlockSpec((None, seq, qw), lambda k, b: (b, 0, k)),
        out_shape=jax.ShapeDtypeStruct((bsz, seq, SW_WIDTH), jnp.bfloat16),
        compiler_params=_params("parallel", "arbitrary"),
        name="sw_attention",
    )(*([proj] * (3 + nz)), bias, sink_col)


def _t5_bucket_index(rel):
    nb = T5_BUCKETS // 2
    max_exact = nb // 2
    ret = (rel > 0).astype(np.int32) * nb
    n = np.abs(rel)
    large = max_exact + (np.log(np.maximum(n, 1) / max_exact)
                         / np.log(T5_MAX_DIST / max_exact) * (nb - max_exact)).astype(np.int32)
    large = np.minimum(large, nb - 1)
    return (ret + np.where(n < max_exact, n, large)).astype(np.int32)


def _sw_bias_table(t5_bias):
    rel = np.arange(3 * SW_BLOCK)[None, :] - SW_BLOCK - np.arange(SW_BLOCK)[:, None]
    band = np.abs(rel) <= SW_WINDOW
    hot = (_t5_bucket_index(rel)[..., None] == np.arange(T5_BUCKETS)).astype(np.float32)
    t = jnp.einsum("qkb,bh->hqk", jnp.asarray(hot), t5_bias.astype(jnp.float32),
                   precision=lax.Precision.HIGHEST)
    t = jnp.where(jnp.asarray(band)[None], t * LOG2E, NEG_INF)
    return t.reshape(SW_HEADS * SW_BLOCK, 3 * SW_BLOCK)


def _mem_kernel(q_ref, z_ref, k_ref, v_ref, o_ref):
    seq = q_ref.shape[0]
    for hh in range(MEM_HEADS_PER_STEP):
        lanes = slice(hh * MEM_HEAD_DIM, (hh + 1) * MEM_HEAD_DIM)
        k = k_ref[:, lanes]
        v = v_ref[:, lanes]
        for i in range(seq // MEM_QBLOCK):
            r0 = i * MEM_QBLOCK
            q = q_ref[r0:r0 + MEM_QBLOCK, lanes]
            s = lax.dot_general(q, k, _NT_DIMS, preferred_element_type=jnp.float32)
            mx = jnp.max(s, axis=-1, keepdims=True)
            e = jnp.exp2(s - mx)
            den = jnp.sum(e, axis=-1, keepdims=True)
            o = jnp.dot(e.astype(v.dtype), v, preferred_element_type=jnp.float32) / den
            z = z_ref[r0:r0 + MEM_QBLOCK, lanes].astype(jnp.float32)
            o_ref[r0:r0 + MEM_QBLOCK, lanes] = (o * _silu_of_double(z)).astype(o_ref.dtype)


def _mem_attention(proj, memkv):
    bsz, seq, _ = proj.shape
    mtok = memkv.shape[1]

    width = MEM_HEADS_PER_STEP * MEM_HEAD_DIM

    def qcol(off):
        return pl.BlockSpec((None, seq, width), lambda h, b: (b, 0, off // width + h))

    def kvcol(off):
        return pl.BlockSpec((None, mtok, width), lambda h, b: (b, 0, off // width + h))

    assert MEM_HEADS % MEM_HEADS_PER_STEP == 0
    assert OFF_QM % width == 0 and OFF_ZM % width == 0 and MEM_WIDTH % width == 0
    return pl.pallas_call(
        _mem_kernel,
        grid=(MEM_HEADS // MEM_HEADS_PER_STEP, bsz),
        in_specs=[qcol(OFF_QM), qcol(OFF_ZM), kvcol(0), kvcol(MEM_WIDTH)],
        out_specs=pl.BlockSpec((None, seq, width), lambda h, b: (b, 0, h)),
        out_shape=jax.ShapeDtypeStruct((bsz, seq, MEM_WIDTH), jnp.bfloat16),
        compiler_params=_params("parallel", "arbitrary"),
        name="mem_attention",
    )(proj, proj, memkv, memkv)


def _merge_kernel(a_ref, b_ref, m_ref, wa_ref, wb_ref, wm_ref, ga_ref, gb_ref, gm_ref, o_ref):
    def branch(x_ref, w_ref, g_ref):
        y = jnp.dot(x_ref[...], w_ref[...], preferred_element_type=jnp.float32)
        return _twice_sigmoid_of_double(g_ref[...].astype(jnp.float32)) * y

    acc = branch(a_ref, wa_ref, ga_ref)
    acc = acc + branch(b_ref, wb_ref, gb_ref)
    acc = acc + branch(m_ref, wm_ref, gm_ref)
    o_ref[...] = (0.5 * acc).astype(o_ref.dtype)


def _merge(a2d, b2d, m2d, wa, wb, wm, layer, proj2d):
    rows = a2d.shape[0]
    bm = min(MERGE_BM, rows)
    bn = MERGE_BN

    def act(width):
        return pl.BlockSpec((bm, width), lambda i, j: (i, 0))

    def wgt(width):
        return pl.BlockSpec((None, width, bn), lambda i, j: (layer, 0, j))

    def gate(off):
        return pl.BlockSpec((bm, bn), lambda i, j: (i, off // bn + j))

    assert OFF_GA % bn == 0 and OFF_GB % bn == 0 and OFF_GM % bn == 0 and rows % bm == 0
    return pl.pallas_call(
        _merge_kernel,
        grid=(rows // bm, D_MODEL // bn),
        in_specs=[act(NA_WIDTH), act(SW_WIDTH), act(MEM_WIDTH),
                  wgt(NA_WIDTH), wgt(SW_WIDTH), wgt(MEM_WIDTH),
                  gate(OFF_GA), gate(OFF_GB), gate(OFF_GM)],
        out_specs=pl.BlockSpec((bm, bn), lambda i, j: (i, j)),
        out_shape=jax.ShapeDtypeStruct((rows, D_MODEL), jnp.bfloat16),
        compiler_params=_params("parallel", "arbitrary"),
        name="branch_merge",
    )(a2d, b2d, m2d, wa, wb, wm, proj2d, proj2d, proj2d)


def _outproj_kernel(*refs, emit_next):
    if emit_next:
        mg_ref, w_ref, x_ref, g_ref, gn_ref, o_ref, xg_ref, ssqn_ref, y_ref, ssq_ref, ssqn_acc = refs
    else:
        mg_ref, w_ref, x_ref, g_ref, o_ref, y_ref, ssq_ref = refs
    i = pl.program_id(0)
    j = pl.program_id(1)
    n_tiles = pl.num_programs(0) - 1
    bn = w_ref.shape[1]
    d = y_ref.shape[1]
    col = pl.multiple_of(j * bn, bn)
    cur = lax.rem(i, 2)

    def finish():
        inv = lax.rsqrt(ssq_ref[1 - cur] * (1.0 / d) + RMS_EPS)
        x_new = x_ref[...] + (y_ref[:, pl.ds(col, bn)] * inv) * g_ref[...]
        o_ref[...] = x_new
        if emit_next:
            xg_ref[...] = (x_new * gn_ref[...]).astype(xg_ref.dtype)
            total = jnp.sum(x_new * x_new, axis=-1, keepdims=True) + jnp.where(j == 0, 0.0, ssqn_acc[...])
            ssqn_acc[...] = total
            ssqn_ref[...] = total

    def project():
        y = jnp.dot(mg_ref[...], w_ref[...], preferred_element_type=jnp.float32)
        y_ref[:, pl.ds(col, bn)] = y
        part = jnp.sum(y * y, axis=-1, keepdims=True)
        ssq_ref[cur] = part + jnp.where(j == 0, 0.0, ssq_ref[cur])

    @pl.when(i == 0)
    def _():
        @pl.when(j == 0)
        def _():
            ssq_ref[...] = jnp.zeros_like(ssq_ref)
            if emit_next:
                ssqn_acc[...] = jnp.zeros_like(ssqn_acc)
        project()

    @pl.when((i > 0) & (i < n_tiles))
    def _():
        finish()
        project()

    @pl.when(i == n_tiles)
    def _():
        finish()


def _outproj(merged2d, w_all, layer, x2d, gain, next_gain=None):
    rows, d = x2d.shape
    bm = min(OUT_BM, rows)
    bn = OUT_BN
    n_tiles = rows // bm

    def finished_block(i, j):
        return (jnp.maximum(i - 1, 0), jnp.where(i == 0, 0, j))

    emit_next = next_gain is not None
    operands = [merged2d, w_all, x2d, gain.reshape(1, d)]
    in_specs = [pl.BlockSpec((bm, d), lambda i, j: (jnp.minimum(i, n_tiles - 1), 0)),
                pl.BlockSpec((None, d, bn), lambda i, j: (layer, 0, jnp.where(i == n_tiles, d // bn - 1, j))),
                pl.BlockSpec((bm, bn), finished_block),
                pl.BlockSpec((1, bn), lambda i, j: (0, j))]
    out_specs = [pl.BlockSpec((bm, bn), finished_block)]
    out_shape = [jax.ShapeDtypeStruct((rows, d), jnp.float32)]
    if emit_next:
        operands.append(next_gain.reshape(1, d))
        in_specs.append(pl.BlockSpec((1, bn), lambda i, j: (0, j)))
        out_specs += [pl.BlockSpec((bm, bn), finished_block),
                      pl.BlockSpec((bm, 1), lambda i, j: (jnp.maximum(i - 1, 0), 0))]
        out_shape += [jax.ShapeDtypeStruct((rows, d), jnp.bfloat16),
                      jax.ShapeDtypeStruct((rows, 1), jnp.float32)]
    outs = pl.pallas_call(
        functools.partial(_outproj_kernel, emit_next=emit_next),
        grid=(n_tiles + 1, d // bn),
        in_specs=in_specs, out_specs=out_specs, out_shape=out_shape,
        scratch_shapes=[pltpu.VMEM((bm, d), jnp.float32),
                        pltpu.VMEM((2, bm, 1), jnp.float32)]
        + ([pltpu.VMEM((bm, 1), jnp.float32)] if emit_next else []),
        compiler_params=_params("arbitrary", "arbitrary"),
        name="out_proj_norm",
    )(*operands)
    return tuple(outs) if emit_next else outs[0]


_SIDE_CAST_WEIGHTS = ("w_mem_kv", "w_branch_a", "w_branch_b", "w_branch_m", "w_out")


def _in_proj_scale():
    s = np.ones((1, IN_WIDTH), np.float32)
    s[0, OFF_QA:OFF_QA + NA_WIDTH] = HEAD_DIM ** -0.5 * LOG2E
    s[0, OFF_QB:OFF_QB + SW_WIDTH] = HEAD_DIM ** -0.5 * LOG2E
    s[0, OFF_QM:OFF_QM + MEM_WIDTH] = MEM_HEAD_DIM ** -0.5 * LOG2E
    for off, width in ((OFF_ZA, NA_WIDTH), (OFF_ZB, SW_WIDTH), (OFF_ZM, MEM_WIDTH),
                       (OFF_GA, D_MODEL), (OFF_GB, D_MODEL), (OFF_GM, D_MODEL)):
        s[0, off:off + width] = 0.5
    return s


def _trunk(x, mem, p):
    bsz, seq, d = x.shape
    mtok = mem.shape[1]
    x2d = x.reshape(bsz * seq, d)
    mem2d = mem.reshape(bsz * mtok, d)
    h, h_ssq = _rmsnorm(x2d, p["pre_norm"][0]), None
    for l in range(DEPTH):
        w_in_bf = p["w_in_bf16"]
        pending = [name for name in _SIDE_CAST_WEIGHTS if p[name].dtype != jnp.bfloat16]
        next_w_in = l + 1 < DEPTH and w_in_bf[l + 1] is None
        outs = _matmul(h, w_in_bf[l], 0, p["in_scale"], row_ssq=h_ssq,
                       cast_src=p["w_in"] if next_w_in else None, cast_layer=l + 1,
                       cast_all=[p[name] for name in pending])
        if next_w_in or pending:
            proj2d, copies = outs[0], list(outs[1:])
            if next_w_in:
                w_in_bf[l + 1] = copies.pop(0)
            p.update(zip(pending, copies))
        else:
            proj2d = outs
        proj = proj2d.reshape(bsz, seq, IN_WIDTH)
        mem_h = _rmsnorm(mem2d, p["mem_norm"][l])
        memkv = _matmul(mem_h, p["w_mem_kv"], l, p["kv_scale"]).reshape(bsz, mtok, 2 * MEM_WIDTH)
        a = _na_attention(proj, p["na_bias"], l)
        b = _sw_attention(proj, p["sw_bias"], p["sink_col"][l])
        m = _mem_attention(proj, memkv)
        merged = _merge(a.reshape(bsz * seq, NA_WIDTH), b.reshape(bsz * seq, SW_WIDTH),
                        m.reshape(bsz * seq, MEM_WIDTH), p["w_branch_a"], p["w_branch_b"],
                        p["w_branch_m"], l, proj2d)
        if l + 1 < DEPTH:
            x2d, h, h_ssq = _outproj(merged, p["w_out"], l, x2d, p["post_norm"][l],
                                     next_gain=p["pre_norm"][l + 1])
        else:
            x2d = _outproj(merged, p["w_out"], l, x2d, p["post_norm"][l])
    return x2d.reshape(bsz, seq, d)


def kernel(x_prompt, x_sample, mem_prompt, mem_sample, pre_norm, post_norm, mem_norm, w_in, w_mem_kv,
           w_branch_a, w_branch_b, w_branch_m, w_out, na_rpb, attn_sink, t5_bias):
    bf16 = jnp.bfloat16
    seq = x_prompt.shape[1]
    assert x_sample.shape[1] == seq and seq % (NA_QROWS * GRID_W) == 0 and seq % SW_BLOCK == 0
    assert seq // GRID_W >= NA_KROWS and seq % MEM_QBLOCK == 0 and seq // SW_BLOCK >= 2
    assert LANES % GRID_W == 0 and NA_KROWS % (LANES // GRID_W) == 0
    sink_col = jnp.repeat(attn_sink.astype(jnp.float32) * LOG2E, SW_BLOCK, axis=1)
    sink_col = jnp.broadcast_to(sink_col[..., None], sink_col.shape + (HEAD_DIM,))
    p = {
        "pre_norm": pre_norm, "post_norm": post_norm, "mem_norm": mem_norm,
        "w_in": w_in, "w_in_bf16": [_cast_layer(w_in, 0)] + [None] * (DEPTH - 1),
        "w_mem_kv": w_mem_kv, "w_branch_a": w_branch_a, "w_branch_b": w_branch_b,
        "w_branch_m": w_branch_m, "w_out": w_out,
        "in_scale": jnp.asarray(_in_proj_scale()),
        "kv_scale": jnp.ones((1, 2 * MEM_WIDTH), jnp.float32),
        "na_bias": _na_bias_tables(na_rpb, seq),
        "sw_bias": _sw_bias_table(t5_bias),
        "sink_col": sink_col,
    }
    y_prompt = _trunk(x_prompt, mem_prompt, p)
    y_sample = _trunk(x_sample, mem_sample, p)
    return (y_prompt, y_sample)
```
